```python
import jax, jax.numpy as jnp
from jax import lax
import numpy as np

D_MODEL = 2048
BATCH = 1
SEQ = 16384
DEPTH = 1

MIX_WIDTH = D_MODEL
CONV_WIDTH = D_MODEL // 2
ATTN_WIDTH = MIX_WIDTH - CONV_WIDTH
HEAD_DIM = 64
N_HEADS = ATTN_WIDTH // HEAD_DIM
N_KV_HEADS = 2
GQA_GROUP = N_HEADS // N_KV_HEADS
KV_WIDTH = N_KV_HEADS * HEAD_DIM
CONV_KERNEL = 31
WINDOW = 128
BLOCK = 128
ROPE_THETA = 500000.0
ROTARY_DIM = HEAD_DIM // 4
RMS_EPS = 1e-6
LN_EPS = 1e-5
NEG_INF = -1e30

SPLIT_SIZES = (ATTN_WIDTH, KV_WIDTH, KV_WIDTH, ATTN_WIDTH,
               CONV_WIDTH, CONV_WIDTH, CONV_WIDTH)
IN_COLS = sum(SPLIT_SIZES)
SPLIT_POINTS = tuple(int(v) for v in np.cumsum(SPLIT_SIZES)[:-1])

kernel_name = "hymba_conformer_swa_sink_hybrid"


def rms_norm(x, g):
    xf = x.astype(jnp.float32)
    y = xf * lax.rsqrt(jnp.mean(xf * xf, axis=-1, keepdims=True) + RMS_EPS)
    return (y * g.astype(jnp.float32)).astype(x.dtype)


def layer_norm(x, g, b):
    xf = x.astype(jnp.float32)
    mu = jnp.mean(xf, axis=-1, keepdims=True)
    var = jnp.mean(jnp.square(xf - mu), axis=-1, keepdims=True)
    y = (xf - mu) * lax.rsqrt(var + LN_EPS)
    return (y * g.astype(jnp.float32) + b.astype(jnp.float32)).astype(x.dtype)


def rope_cos_sin(positions):
    inv_freq = ROPE_THETA ** (-jnp.arange(0, ROTARY_DIM, 2, dtype=jnp.float32) / ROTARY_DIM)
    ang = positions.astype(jnp.float32)[..., None] * inv_freq
    return jnp.cos(ang)[:, :, None, :], jnp.sin(ang)[:, :, None, :]


def partial_rope(x, cos, sin):
    half = ROTARY_DIM // 2
    cos = cos.astype(x.dtype)
    sin = sin.astype(x.dtype)
    x1 = x[..., :half]
    x2 = x[..., half:ROTARY_DIM]
    return jnp.concatenate([x1 * cos - x2 * sin, x2 * cos + x1 * sin, x[..., ROTARY_DIM:]], axis=-1)


def sliding_window_attention(q, k, v, sinks):
    B, S = q.shape[0], q.shape[1]
    nb = S // BLOCK
    qb = q.reshape(B, nb, BLOCK, N_KV_HEADS, GQA_GROUP, HEAD_DIM)
    kb = k.reshape(B, nb, BLOCK, N_KV_HEADS, HEAD_DIM)
    vb = v.reshape(B, nb, BLOCK, N_KV_HEADS, HEAD_DIM)
    pad = ((0, 0), (1, 0), (0, 0), (0, 0), (0, 0))
    kw = jnp.concatenate([jnp.pad(kb, pad)[:, :-1], kb], axis=2)
    vw = jnp.concatenate([jnp.pad(vb, pad)[:, :-1], vb], axis=2)
    scale = HEAD_DIM ** -0.5
    scores = jnp.einsum('bnqkgd,bnskd->bnkgqs', qb, kw).astype(jnp.float32) * scale
    q_pos = jnp.arange(BLOCK)[:, None] + BLOCK
    k_pos = jnp.arange(2 * BLOCK)[None, :]
    rel = q_pos - k_pos
    band = (rel >= 0) & (rel < WINDOW)
    has_prev = (jnp.arange(nb)[:, None, None] > 0) | (k_pos[None] >= BLOCK)
    mask = band[None] & has_prev
    scores = jnp.where(mask[None, :, None, None], scores, NEG_INF)
    sink = sinks.astype(jnp.float32).reshape(1, 1, N_KV_HEADS, GQA_GROUP, 1, 1)
    m = jnp.maximum(jnp.max(scores, axis=-1, keepdims=True), sink)
    e = jnp.exp(scores - m)
    denom = jnp.sum(e, axis=-1, keepdims=True) + jnp.exp(sink - m)
    probs = (e / denom).astype(v.dtype)
    out = jnp.einsum('bnkgqs,bnskd->bnqkgd', probs, vw)
    return out.reshape(B, S, N_HEADS * HEAD_DIM)


def conformer_conv(val, gate, w_dw, b_dw, ln_g, ln_b, w_pw, b_pw):
    h = val * jax.nn.sigmoid(gate)
    h = lax.conv_general_dilated(
        h, w_dw[:, None, :].astype(h.dtype), window_strides=(1,),
        padding=[(CONV_KERNEL - 1, 0)], dimension_numbers=('NWC', 'WIO', 'NWC'),
        feature_group_count=CONV_WIDTH) + b_dw
    h = jax.nn.silu(layer_norm(h, ln_g, ln_b))
    return h @ w_pw + b_pw


def setup_inputs(seed: int = 0) -> dict:
    key = jax.random.key(seed)
    ks = jax.random.split(key, 16)
    f32 = jnp.float32
    x = jax.random.normal(ks[0], (BATCH, SEQ, D_MODEL), f32)
    positions = jnp.broadcast_to(jnp.arange(SEQ, dtype=jnp.int32), (BATCH, SEQ))
    pre_norm_g = 1.0 + 0.05 * jax.random.normal(ks[1], (DEPTH, D_MODEL), f32)
    w_in = jax.random.normal(ks[2], (DEPTH, D_MODEL, IN_COLS), f32) * D_MODEL ** -0.5
    b_in = 0.02 * jax.random.normal(ks[3], (DEPTH, IN_COLS), f32)
    sinks = jax.random.normal(ks[4], (DEPTH, N_HEADS), f32)
    w_dw = jax.random.normal(ks[5], (DEPTH, CONV_KERNEL, CONV_WIDTH), f32) * CONV_KERNEL ** -0.5
    b_dw = 0.02 * jax.random.normal(ks[6], (DEPTH, CONV_WIDTH), f32)
    conv_ln_g = 1.0 + 0.05 * jax.random.normal(ks[7], (DEPTH, CONV_WIDTH), f32)
    conv_ln_b = 0.02 * jax.random.normal(ks[8], (DEPTH, CONV_WIDTH), f32)
    w_pw = jax.random.normal(ks[9], (DEPTH, CONV_WIDTH, CONV_WIDTH), f32) * CONV_WIDTH ** -0.5
    b_pw = 0.02 * jax.random.normal(ks[10], (DEPTH, CONV_WIDTH), f32)
    w_out = jax.random.normal(ks[11], (DEPTH, MIX_WIDTH, D_MODEL), f32) * MIX_WIDTH ** -0.5
    b_out = 0.02 * jax.random.normal(ks[12], (DEPTH, D_MODEL), f32)
    post_norm_g = 1.0 + 0.05 * jax.random.normal(ks[13], (DEPTH, D_MODEL), f32)
    return {"x": x, "positions": positions, "pre_norm_g": pre_norm_g, "w_in": w_in, "b_in": b_in,
            "sinks": sinks, "w_dw": w_dw, "b_dw": b_dw, "conv_ln_g": conv_ln_g, "conv_ln_b": conv_ln_b,
            "w_pw": w_pw, "b_pw": b_pw, "w_out": w_out, "b_out": b_out, "post_norm_g": post_norm_g}


def reference(x, positions, pre_norm_g, w_in, b_in, sinks, w_dw, b_dw, conv_ln_g, conv_ln_b,
              w_pw, b_pw, w_out, b_out, post_norm_g):
    B, S = x.shape[0], x.shape[1]
    cos, sin = rope_cos_sin(positions)
    for l in range(DEPTH):
        h = rms_norm(x, pre_norm_g[l])
        p = h @ w_in[l] + b_in[l]
        q, k, v, g_attn, glu_val, glu_gate, g_conv = jnp.split(p, SPLIT_POINTS, axis=-1)
        q = partial_rope(q.reshape(B, S, N_HEADS, HEAD_DIM), cos, sin)
        k = partial_rope(k.reshape(B, S, N_KV_HEADS, HEAD_DIM), cos, sin)
        v = v.reshape(B, S, N_KV_HEADS, HEAD_DIM)
        attn = sliding_window_attention(q, k, v, sinks[l]) * jax.nn.silu(g_attn)
        conv = conformer_conv(glu_val, glu_gate, w_dw[l], b_dw[l], conv_ln_g[l], conv_ln_b[l],
                              w_pw[l], b_pw[l]) * jax.nn.silu(g_conv)
        y = jnp.concatenate([attn, conv], axis=-1) @ w_out[l] + b_out[l]
        x = x + rms_norm(y, post_norm_g[l])
    return x
```

```python
import functools

import jax
import jax.numpy as jnp
from jax import lax
from jax.experimental import pallas as pl
from jax.experimental.pallas import tpu as pltpu

D_MODEL = 2048
CONV_WIDTH = 1024
ATTN_WIDTH = 1024
HEAD_DIM = 64
N_HEADS = 16
N_KV_HEADS = 2
GQA_GROUP = N_HEADS // N_KV_HEADS
KV_WIDTH = N_KV_HEADS * HEAD_DIM
CONV_KERNEL = 31
WINDOW = 128
ROPE_THETA = 500000.0
ROTARY_DIM = HEAD_DIM // 4
ROTARY_HALF = ROTARY_DIM // 2
RMS_EPS = 1e-6
LN_EPS = 1e-5
NEG_INF = -1e30
IN_COLS = 2 * ATTN_WIDTH + 2 * KV_WIDTH + 3 * CONV_WIDTH

SECTION = 1024
SEC_Q, SEC_GATTN, SEC_VAL, SEC_GATE, SEC_GCONV = 0, 1, 2, 3, 4
KV_COL0 = 5 * SECTION

LANES = 128
SUBLANES = 8
CONV_HALO = 32
VMEM_LIMIT = 56 * 1024 * 1024

F32 = jnp.float32
BF16 = jnp.bfloat16


def _in_proj_kernel(x_ref, g_ref, w_ref, b_ref, o_ref, h_ref, *, row_chunk):
    @pl.when(pl.program_id(1) == 0)
    def _():
        def body(c, carry):
            r0 = pl.multiple_of(c * row_chunk, row_chunk)
            x = x_ref[pl.ds(r0, row_chunk), :]
            ms = jnp.mean(x * x, axis=-1, keepdims=True)
            h_ref[pl.ds(r0, row_chunk), :] = (x * lax.rsqrt(ms + RMS_EPS) * g_ref[...]).astype(BF16)
            return carry
        lax.fori_loop(0, x_ref.shape[0] // row_chunk, body, 0)

    o_ref[...] = jnp.dot(h_ref[...], w_ref[...], preferred_element_type=F32) + b_ref[...]


def _in_proj(x2d, g, w, b, *, tm, tn):
    s, d = x2d.shape
    n = w.shape[1]
    return pl.pallas_call(
        functools.partial(_in_proj_kernel, row_chunk=128),
        grid=(s // tm, n // tn),
        in_specs=[
            pl.BlockSpec((tm, d), lambda i, j: (i, 0)),
            pl.BlockSpec((1, d), lambda i, j: (0, 0)),
            pl.BlockSpec((d, tn), lambda i, j: (0, j)),
            pl.BlockSpec((1, tn), lambda i, j: (0, j)),
        ],
        out_specs=pl.BlockSpec((tm, tn), lambda i, j: (i, j)),
        out_shape=jax.ShapeDtypeStruct((s, n), F32),
        scratch_shapes=[pltpu.VMEM((tm, d), BF16)],
        compiler_params=pltpu.CompilerParams(
            dimension_semantics=("parallel", "arbitrary"), vmem_limit_bytes=VMEM_LIMIT),
        name="in_proj",
    )(x2d, g, w, b)


def _rope_tables(pos, invf):
    ang = pos * invf
    d = lax.broadcasted_iota(jnp.int32, ang.shape, 1) % HEAD_DIM
    c = jnp.cos(ang)
    s = jnp.sin(ang)
    cos_t = jnp.where(d < ROTARY_DIM, c, 1.0)
    sin_t = jnp.where(d < ROTARY_HALF, -s, jnp.where(d < ROTARY_DIM, s, 0.0))
    return cos_t, sin_t, d


def _rope(xc, cos_t, sin_t, d):
    partner = jnp.where(d < ROTARY_HALF,
                        pltpu.roll(xc, LANES - ROTARY_HALF, 1),
                        pltpu.roll(xc, ROTARY_HALF, 1))
    return xc * cos_t + partner * sin_t


def _mixers_kernel(pos_ref, posp_ref, invf_ref, q_ref, ga_ref, val_ref, gate_ref, gc_ref, kv_ref,
                   kvp_ref, valp_ref, gatep_ref, sinks_ref, wdw_ref, bdw_ref, lng_ref, lnb_ref,
                   wpw_ref, bpw_ref, u_ref, qbuf, kbuf, vbuf, hbuf, cbuf, *, tile):
    i = pl.program_id(0)
    nblk = tile // WINDOW
    invf = invf_ref[...]

    cos_t, sin_t, d = _rope_tables(pos_ref[...], invf)
    scale = HEAD_DIM ** -0.5
    for c in range(ATTN_WIDTH // LANES):
        qc = q_ref[:, c * LANES:(c + 1) * LANES]
        qbuf[:, c * LANES:(c + 1) * LANES] = (_rope(qc, cos_t, sin_t, d) * scale).astype(BF16)
    kbuf[WINDOW:, :] = _rope(kv_ref[:, :KV_WIDTH], cos_t, sin_t, d).astype(BF16)
    vbuf[WINDOW:, :] = kv_ref[:, KV_WIDTH:].astype(BF16)
    cos_p, sin_p, dp = _rope_tables(posp_ref[...], invf)
    kbuf[:WINDOW, :] = _rope(kvp_ref[:, :KV_WIDTH], cos_p, sin_p, dp).astype(BF16)
    vbuf[:WINDOW, :] = kvp_ref[:, KV_WIDTH:].astype(BF16)

    qi = lax.broadcasted_iota(jnp.int32, (WINDOW, 2 * WINDOW), 0) + WINDOW
    kj = lax.broadcasted_iota(jnp.int32, (WINDOW, 2 * WINDOW), 1)
    rel = qi - kj
    band = (rel >= 0) & (rel < WINDOW)
    for b in range(nblk):
        has_prev = (kj >= WINDOW) | (i * nblk + b > 0)
        mask = band & has_prev
        r0 = b * WINDOW
        for hp in range(N_HEADS // 2):
            outs = []
            for h in (2 * hp, 2 * hp + 1):
                g = h // GQA_GROUP
                qh = qbuf[r0:r0 + WINDOW, h * HEAD_DIM:(h + 1) * HEAD_DIM]
                kc = kbuf[r0:r0 + 2 * WINDOW, g * HEAD_DIM:(g + 1) * HEAD_DIM]
                vc = vbuf[r0:r0 + 2 * WINDOW, g * HEAD_DIM:(g + 1) * HEAD_DIM]
                s = lax.dot_general(qh, kc, (((1,), (1,)), ((), ())), preferred_element_type=F32)
                s = jnp.where(mask, s, NEG_INF)
                sink = sinks_ref[h]
                m = jnp.maximum(jnp.max(s, axis=-1, keepdims=True), sink)
                e = jnp.exp(s - m)
                denom = jnp.sum(e, axis=-1, keepdims=True) + jnp.exp(sink - m)
                p = (e / denom).astype(BF16)
                outs.append(jnp.dot(p, vc, preferred_element_type=F32))
            o2 = jnp.concatenate(outs, axis=1)
            ga = ga_ref[r0:r0 + WINDOW, hp * LANES:(hp + 1) * LANES]
            u_ref[r0:r0 + WINDOW, hp * LANES:(hp + 1) * LANES] = (o2 * jax.nn.silu(ga)).astype(BF16)

    hbuf[CONV_HALO:, :] = val_ref[...] * jax.nn.sigmoid(gate_ref[...])
    hprev = valp_ref[...] * jax.nn.sigmoid(gatep_ref[...])
    hbuf[:CONV_HALO, :] = jnp.where(i > 0, hprev, 0.0)

    lead = CONV_HALO - (CONV_KERNEL - 1)

    def conv_chunk(c, carry):
        r0 = pl.multiple_of(c * SUBLANES, SUBLANES)
        win = hbuf[pl.ds(r0, CONV_HALO + SUBLANES), :]
        acc = jnp.zeros((SUBLANES, CONV_WIDTH), F32)
        for j in range(CONV_KERNEL):
            acc = acc + win[lead + j:lead + j + SUBLANES, :] * wdw_ref[j:j + 1, :]
        acc = acc + bdw_ref[...]
        mu = jnp.mean(acc, axis=-1, keepdims=True)
        xc = acc - mu
        var = jnp.mean(xc * xc, axis=-1, keepdims=True)
        y = xc * lax.rsqrt(var + LN_EPS) * lng_ref[...] + lnb_ref[...]
        cbuf[pl.ds(r0, SUBLANES), :] = jax.nn.silu(y)
        return carry

    lax.fori_loop(0, tile // SUBLANES, conv_chunk, 0)

    pw = jnp.dot(cbuf[...].astype(BF16), wpw_ref[...], preferred_element_type=F32) + bpw_ref[...]
    u_ref[:, ATTN_WIDTH:] = (pw * jax.nn.silu(gc_ref[...])).astype(BF16)


def _mixers(p, pos, invf, sinks, w_dw, b_dw, ln_g, ln_b, w_pw, b_pw, *, tile):
    s = p.shape[0]
    blk_per_tile = tile // WINDOW
    halo_per_tile = tile // CONV_HALO
    kv_blk = KV_COL0 // (2 * KV_WIDTH)

    def sec(k):
        return pl.BlockSpec((tile, SECTION), lambda i, k=k: (i, k))

    def full(a):
        return pl.BlockSpec(a.shape, lambda i: (0,) * a.ndim)

    def prev_block(i):
        return jnp.maximum(i * blk_per_tile - 1, 0)

    def prev_halo(i):
        return jnp.maximum(i * halo_per_tile - 1, 0)

    return pl.pallas_call(
        functools.partial(_mixers_kernel, tile=tile),
        grid=(s // tile,),
        in_specs=[
            pl.BlockSpec((tile, 1), lambda i: (i, 0)),
            pl.BlockSpec((WINDOW, 1), lambda i: (prev_block(i), 0)),
            full(invf),
            sec(SEC_Q), sec(SEC_GATTN), sec(SEC_VAL), sec(SEC_GATE), sec(SEC_GCONV),
            pl.BlockSpec((tile, 2 * KV_WIDTH), lambda i: (i, kv_blk)),
            pl.BlockSpec((WINDOW, 2 * KV_WIDTH), lambda i: (prev_block(i), kv_blk)),
            pl.BlockSpec((CONV_HALO, SECTION), lambda i: (prev_halo(i), SEC_VAL)),
            pl.BlockSpec((CONV_HALO, SECTION), lambda i: (prev_halo(i), SEC_GATE)),
            pl.BlockSpec(memory_space=pltpu.SMEM),
            full(w_dw), full(b_dw), full(ln_g), full(ln_b), full(w_pw), full(b_pw),
        ],
        out_specs=pl.BlockSpec((tile, D_MODEL), lambda i: (i, 0)),
        out_shape=jax.ShapeDtypeStruct((s, D_MODEL), BF16),
        scratch_shapes=[
            pltpu.VMEM((tile, ATTN_WIDTH), BF16),
            pltpu.VMEM((WINDOW + tile, KV_WIDTH), BF16),
            pltpu.VMEM((WINDOW + tile, KV_WIDTH), BF16),
            pltpu.VMEM((CONV_HALO + tile, CONV_WIDTH), F32),
            pltpu.VMEM((tile, CONV_WIDTH), F32),
        ],
        compiler_params=pltpu.CompilerParams(
            dimension_semantics=("parallel",), vmem_limit_bytes=VMEM_LIMIT),
        name="mixers",
    )(pos, pos, invf, p, p, p, p, p, p, p, p, p, sinks, w_dw, b_dw, ln_g, ln_b, w_pw, b_pw)


def _out_proj_kernel(u_ref, x_ref, w_ref, b_ref, g_ref, o_ref):
    y = jnp.dot(u_ref[...], w_ref[...], preferred_element_type=F32) + b_ref[...]
    ms = jnp.mean(y * y, axis=-1, keepdims=True)
    o_ref[...] = x_ref[...] + y * lax.rsqrt(ms + RMS_EPS) * g_ref[...]


def _out_proj(u, x2d, w, b, g, *, tm):
    s, d = x2d.shape
    return pl.pallas_call(
        _out_proj_kernel,
        grid=(s // tm,),
        in_specs=[
            pl.BlockSpec((tm, u.shape[1]), lambda i: (i, 0)),
            pl.BlockSpec((tm, d), lambda i: (i, 0)),
            pl.BlockSpec(w.shape, lambda i: (0, 0)),
            pl.BlockSpec((1, d), lambda i: (0, 0)),
            pl.BlockSpec((1, d), lambda i: (0, 0)),
        ],
        out_specs=pl.BlockSpec((tm, d), lambda i: (i, 0)),
        out_shape=jax.ShapeDtypeStruct((s, d), F32),
        compiler_params=pltpu.CompilerParams(
            dimension_semantics=("parallel",), vmem_limit_bytes=VMEM_LIMIT),
        name="out_proj",
    )(u, x2d, w, b, g)


def _regroup_cols(a):
    q_end = ATTN_WIDTH
    kv_end = q_end + 2 * KV_WIDTH
    return jnp.concatenate([a[..., :q_end], a[..., kv_end:], a[..., q_end:kv_end]], axis=-1)


def kernel(x, positions, pre_norm_g, w_in, b_in, sinks, w_dw, b_dw, conv_ln_g, conv_ln_b,
           w_pw, b_pw, w_out, b_out, post_norm_g):
    bsz, seq, d = x.shape
    depth = w_in.shape[0]
    assert d == D_MODEL and w_in.shape[2] == IN_COLS and seq % 1024 == 0
    lane = jnp.arange(LANES) % HEAD_DIM % ROTARY_HALF
    invf = (ROPE_THETA ** (-(2 * lane).astype(F32) / ROTARY_DIM)).reshape(1, LANES)

    outs = []
    for bi in range(bsz):
        xb = x[bi]
        pos = positions[bi].astype(F32).reshape(seq, 1)
        for l in range(depth):
            w_in_l = _regroup_cols(w_in[l]).astype(BF16)
            b_in_l = _regroup_cols(b_in[l]).reshape(1, IN_COLS)
            p = _in_proj(xb, pre_norm_g[l].reshape(1, d), w_in_l, b_in_l, tm=1024, tn=768)
            u = _mixers(p, pos, invf, sinks[l], w_dw[l], b_dw[l].reshape(1, -1),
                        conv_ln_g[l].reshape(1, -1), conv_ln_b[l].reshape(1, -1),
                        w_pw[l].astype(BF16), b_pw[l].reshape(1, -1), tile=256)
            xb = _out_proj(u, xb, w_out[l].astype(BF16), b_out[l].reshape(1, d),
                           post_norm_g[l].reshape(1, d), tm=512)
        outs.append(xb)
    return outs[0].reshape(1, seq, d) if bsz == 1 else jnp.stack(outs, axis=0)
```

```python
import functools

import jax
import jax.numpy as jnp
from jax import lax
from jax.experimental import pallas as pl
from jax.experimental.pallas import tpu as pltpu

D_MODEL = 2048
CONV_WIDTH = 1024
ATTN_WIDTH = 1024
HEAD_DIM = 64
N_HEADS = 16
N_KV_HEADS = 2
GQA_GROUP = N_HEADS // N_KV_HEADS
KV_WIDTH = N_KV_HEADS * HEAD_DIM
CONV_KERNEL = 31
WINDOW = 128
ROPE_THETA = 500000.0
ROTARY_DIM = HEAD_DIM // 4
ROTARY_HALF = ROTARY_DIM // 2
RMS_EPS = 1e-6
LN_EPS = 1e-5
NEG_INF = -1e30
IN_COLS = 2 * ATTN_WIDTH + 2 * KV_WIDTH + 3 * CONV_WIDTH

SECTION = 1024
SEC_Q, SEC_GATTN, SEC_VAL, SEC_GATE, SEC_GCONV = 0, 1, 2, 3, 4
KV_COL0 = 5 * SECTION

LANES = 128
SUBLANES = 8
CONV_HALO = 32
VMEM_LIMIT = 56 * 1024 * 1024

F32 = jnp.float32
BF16 = jnp.bfloat16


def _in_proj_kernel(x_ref, g_ref, w_ref, b_ref, o_ref, h_ref, *, row_chunk):
    @pl.when(pl.program_id(1) == 0)
    def _():
        def body(c, carry):
            r0 = pl.multiple_of(c * row_chunk, row_chunk)
            x = x_ref[pl.ds(r0, row_chunk), :]
            ms = jnp.mean(x * x, axis=-1, keepdims=True)
            h_ref[pl.ds(r0, row_chunk), :] = (x * lax.rsqrt(ms + RMS_EPS) * g_ref[...]).astype(BF16)
            return carry
        lax.fori_loop(0, x_ref.shape[0] // row_chunk, body, 0)

    o_ref[...] = jnp.dot(h_ref[...], w_ref[...], preferred_element_type=F32) + b_ref[...]


def _in_proj(x2d, g, w, b, *, tm, tn):
    s, d = x2d.shape
    n = w.shape[1]
    return pl.pallas_call(
        functools.partial(_in_proj_kernel, row_chunk=128),
        grid=(s // tm, n // tn),
        in_specs=[
            pl.BlockSpec((tm, d), lambda i, j: (i, 0)),
            pl.BlockSpec((1, d), lambda i, j: (0, 0)),
            pl.BlockSpec((d, tn), lambda i, j: (0, j)),
            pl.BlockSpec((1, tn), lambda i, j: (0, j)),
        ],
        out_specs=pl.BlockSpec((tm, tn), lambda i, j: (i, j)),
        out_shape=jax.ShapeDtypeStruct((s, n), F32),
        scratch_shapes=[pltpu.VMEM((tm, d), BF16)],
        compiler_params=pltpu.CompilerParams(
            dimension_semantics=("parallel", "arbitrary"), vmem_limit_bytes=VMEM_LIMIT),
        name="in_proj",
    )(x2d, g, w, b)


def _rope_tables(pos, invf):
    ang = pos * invf
    d = lax.broadcasted_iota(jnp.int32, ang.shape, 1) % HEAD_DIM
    c = jnp.cos(ang)
    s = jnp.sin(ang)
    cos_t = jnp.where(d < ROTARY_DIM, c, 1.0)
    sin_t = jnp.where(d < ROTARY_HALF, -s, jnp.where(d < ROTARY_DIM, s, 0.0))
    return cos_t, sin_t, d


def _rope(xc, cos_t, sin_t, d):
    partner = jnp.where(d < ROTARY_HALF,
                        pltpu.roll(xc, LANES - ROTARY_HALF, 1),
                        pltpu.roll(xc, ROTARY_HALF, 1))
    return xc * cos_t + partner * sin_t


def _head_pad_variants(a):
    lo = lax.broadcasted_iota(jnp.int32, a.shape, 1) < HEAD_DIM
    r = pltpu.roll(a, HEAD_DIM, 1)
    return (jnp.where(lo, a, 0.0), jnp.where(lo, 0.0, r),
            jnp.where(lo, r, 0.0), jnp.where(lo, 0.0, a))


def _mixers_kernel(pos_ref, posp_ref, invf_ref, q_ref, ga_ref, val_ref, gate_ref, gc_ref, kv_ref,
                   kvp_ref, valp_ref, gatep_ref, sinks_ref, wdw_ref, bdw_ref, lng_ref, lnb_ref,
                   wpw_ref, bpw_ref, u_ref, qbuf, kbuf, vbuf, hs, cb, cbuf, *, tile, conv_rows):
    i = pl.program_id(0)
    nblk = tile // WINDOW
    pairs = GQA_GROUP // 2
    invf = invf_ref[...]

    cos_t, sin_t, d = _rope_tables(pos_ref[...], invf)
    scale = HEAD_DIM ** -0.5
    for c in range(ATTN_WIDTH // LANES):
        qc = (_rope(q_ref[:, c * LANES:(c + 1) * LANES], cos_t, sin_t, d) * scale).astype(BF16)
        for b in range(nblk):
            dst = (b * N_KV_HEADS * pairs + c) * WINDOW
            qbuf[dst:dst + WINDOW, :] = qc[b * WINDOW:(b + 1) * WINDOW, :]
    cos_p, sin_p, dp = _rope_tables(posp_ref[...], invf)
    k_prev = _head_pad_variants(_rope(kvp_ref[:, :KV_WIDTH], cos_p, sin_p, dp))
    k_cur = _head_pad_variants(_rope(kv_ref[:, :KV_WIDTH], cos_t, sin_t, d))
    v_prev = _head_pad_variants(kvp_ref[:, KV_WIDTH:])
    v_cur = _head_pad_variants(kv_ref[:, KV_WIDTH:])
    for n in range(2 * N_KV_HEADS):
        kbuf[n, :WINDOW, :] = k_prev[n].astype(BF16)
        kbuf[n, WINDOW:, :] = k_cur[n].astype(BF16)
        vbuf[n, :WINDOW, :] = v_prev[n].astype(BF16)
        vbuf[n, WINDOW:, :] = v_cur[n].astype(BF16)

    tri = (lax.broadcasted_iota(jnp.int32, (WINDOW, WINDOW), 1)
           <= lax.broadcasted_iota(jnp.int32, (WINDOW, WINDOW), 0))
    nt = (((1,), (1,)), ((), ()))
    for b in range(nblk):
        r0 = b * WINDOW
        for g in range(N_KV_HEADS):
            q0 = (b * N_KV_HEADS + g) * pairs * WINDOW
            q4 = qbuf[q0:q0 + pairs * WINDOW, :]
            s_slot = [lax.dot_general(q4, kbuf[2 * g + e, r0:r0 + 2 * WINDOW, :], nt,
                                      preferred_element_type=F32) for e in range(2)]
            p_slot = [[], []]
            for pr in range(pairs):
                for e in range(2):
                    h = g * GQA_GROUP + 2 * pr + e
                    s_prev = s_slot[e][pr * WINDOW:(pr + 1) * WINDOW, :WINDOW]
                    s_own = s_slot[e][pr * WINDOW:(pr + 1) * WINDOW, WINDOW:]
                    if b == 0:
                        s_prev = jnp.where(i > 0, s_prev, NEG_INF)
                    s = jnp.where(tri, s_own, s_prev)
                    sink = sinks_ref[h]
                    m = jnp.maximum(jnp.max(s, axis=-1, keepdims=True), sink)
                    ex = jnp.exp(s - m)
                    denom = jnp.sum(ex, axis=-1, keepdims=True) + jnp.exp(sink - m)
                    prob = ex * (1.0 / denom)
                    p_slot[e].append(jnp.concatenate(
                        [jnp.where(tri, 0.0, prob), jnp.where(tri, prob, 0.0)], axis=1).astype(BF16))
            o = None
            for e in range(2):
                pe = jnp.concatenate(p_slot[e], axis=0)
                oe = jnp.dot(pe, vbuf[2 * g + e, r0:r0 + 2 * WINDOW, :], preferred_element_type=F32)
                o = oe if o is None else o + oe
            for pr in range(pairs):
                c = g * pairs + pr
                ga = ga_ref[r0:r0 + WINDOW, c * LANES:(c + 1) * LANES]
                u_ref[r0:r0 + WINDOW, c * LANES:(c + 1) * LANES] = (
                    o[pr * WINDOW:(pr + 1) * WINDOW, :] * jax.nn.silu(ga)).astype(BF16)

    rows = CONV_HALO + tile
    lead = CONV_HALO - (CONV_KERNEL - 1)
    for c in range(CONV_WIDTH // LANES):
        cs = slice(c * LANES, (c + 1) * LANES)
        h_prev = jnp.where(i > 0, valp_ref[:, cs] * jax.nn.sigmoid(gatep_ref[:, cs]), 0.0)
        h_full = jnp.concatenate([h_prev, val_ref[:, cs] * jax.nn.sigmoid(gate_ref[:, cs])], axis=0)
        hs[0, :, cs] = h_full
        for s in range(1, SUBLANES):
            hs[s, :, cs] = pltpu.roll(h_full, rows - s, 0)

    for c in range(CONV_WIDTH // LANES):
        cs = slice(c * LANES, (c + 1) * LANES)
        w_taps = [jnp.broadcast_to(wdw_ref[j:j + 1, cs], (SUBLANES, LANES)) for j in range(CONV_KERNEL)]
        bias = jnp.broadcast_to(bdw_ref[:, cs], (SUBLANES, LANES))

        def conv_rows_body(k, carry, cs=cs, w_taps=w_taps, bias=bias):
            r0 = pl.multiple_of(k * conv_rows, conv_rows)
            for v in range(conv_rows // SUBLANES):
                acc = bias
                for j in range(CONV_KERNEL):
                    a, s = divmod(lead + j, SUBLANES)
                    acc = acc + hs[s, pl.ds(r0 + (v + a) * SUBLANES, SUBLANES), cs] * w_taps[j]
                cb[pl.ds(r0 + v * SUBLANES, SUBLANES), cs] = acc
            return carry

        lax.fori_loop(0, tile // conv_rows, conv_rows_body, 0)

    ln_rows = 64

    def ln_body(k, carry):
        r0 = pl.multiple_of(k * ln_rows, ln_rows)
        x = cb[pl.ds(r0, ln_rows), :]
        mu = jnp.mean(x, axis=-1, keepdims=True)
        xc = x - mu
        var = jnp.mean(xc * xc, axis=-1, keepdims=True)
        y = xc * lax.rsqrt(var + LN_EPS) * lng_ref[...] + lnb_ref[...]
        cbuf[pl.ds(r0, ln_rows), :] = jax.nn.silu(y).astype(BF16)
        return carry

    lax.fori_loop(0, tile // ln_rows, ln_body, 0)

    pw = jnp.dot(cbuf[...], wpw_ref[...], preferred_element_type=F32) + bpw_ref[...]
    u_ref[:, ATTN_WIDTH:] = (pw * jax.nn.silu(gc_ref[...])).astype(BF16)


def _mixers(p, pos, invf, sinks, w_dw, b_dw, ln_g, ln_b, w_pw, b_pw, *, tile):
    s = p.shape[0]
    blk_per_tile = tile // WINDOW
    halo_per_tile = tile // CONV_HALO
    kv_blk = KV_COL0 // (2 * KV_WIDTH)

    def sec(k):
        return pl.BlockSpec((tile, SECTION), lambda i, k=k: (i, k))

    def full(a):
        return pl.BlockSpec(a.shape, lambda i: (0,) * a.ndim)

    def prev_block(i):
        return jnp.maximum(i * blk_per_tile - 1, 0)

    def prev_halo(i):
        return jnp.maximum(i * halo_per_tile - 1, 0)

    return pl.pallas_call(
        functools.partial(_mixers_kernel, tile=tile, conv_rows=64),
        grid=(s // tile,),
        in_specs=[
            pl.BlockSpec((tile, 1), lambda i: (i, 0)),
            pl.BlockSpec((WINDOW, 1), lambda i: (prev_block(i), 0)),
            full(invf),
            sec(SEC_Q), sec(SEC_GATTN), sec(SEC_VAL), sec(SEC_GATE), sec(SEC_GCONV),
            pl.BlockSpec((tile, 2 * KV_WIDTH), lambda i: (i, kv_blk)),
            pl.BlockSpec((WINDOW, 2 * KV_WIDTH), lambda i: (prev_block(i), kv_blk)),
            pl.BlockSpec((CONV_HALO, SECTION), lambda i: (prev_halo(i), SEC_VAL)),
            pl.BlockSpec((CONV_HALO, SECTION), lambda i: (prev_halo(i), SEC_GATE)),
            pl.BlockSpec(memory_space=pltpu.SMEM),
            full(w_dw), full(b_dw), full(ln_g), full(ln_b), full(w_pw), full(b_pw),
        ],
        out_specs=pl.BlockSpec((tile, D_MODEL), lambda i: (i, 0)),
        out_shape=jax.ShapeDtypeStruct((s, D_MODEL), BF16),
        scratch_shapes=[
            pltpu.VMEM((tile * ATTN_WIDTH // LANES, LANES), BF16),
            pltpu.VMEM((2 * N_KV_HEADS, WINDOW + tile, LANES), BF16),
            pltpu.VMEM((2 * N_KV_HEADS, WINDOW + tile, LANES), BF16),
            pltpu.VMEM((SUBLANES, CONV_HALO + tile, CONV_WIDTH), F32),
            pltpu.VMEM((tile, CONV_WIDTH), F32),
            pltpu.VMEM((tile, CONV_WIDTH), BF16),
        ],
        compiler_params=pltpu.CompilerParams(
            dimension_semantics=("parallel",), vmem_limit_bytes=VMEM_LIMIT),
        name="mixers",
    )(pos, pos, invf, p, p, p, p, p, p, p, p, p, sinks, w_dw, b_dw, ln_g, ln_b, w_pw, b_pw)


def _out_proj_kernel(u_ref, x_ref, w_ref, b_ref, g_ref, o_ref):
    y = jnp.dot(u_ref[...], w_ref[...], preferred_element_type=F32) + b_ref[...]
    ms = jnp.mean(y * y, axis=-1, keepdims=True)
    o_ref[...] = x_ref[...] + y * lax.rsqrt(ms + RMS_EPS) * g_ref[...]


def _out_proj(u, x2d, w, b, g, *, tm):
    s, d = x2d.shape
    return pl.pallas_call(
        _out_proj_kernel,
        grid=(s // tm,),
        in_specs=[
            pl.BlockSpec((tm, u.shape[1]), lambda i: (i, 0)),
            pl.BlockSpec((tm, d), lambda i: (i, 0)),
            pl.BlockSpec(w.shape, lambda i: (0, 0)),
            pl.BlockSpec((1, d), lambda i: (0, 0)),
            pl.BlockSpec((1, d), lambda i: (0, 0)),
        ],
        out_specs=pl.BlockSpec((tm, d), lambda i: (i, 0)),
        out_shape=jax.ShapeDtypeStruct((s, d), F32),
        compiler_params=pltpu.CompilerParams(
            dimension_semantics=("parallel",), vmem_limit_bytes=VMEM_LIMIT),
        name="out_proj",
    )(u, x2d, w, b, g)


def _regroup_cols(a):
    q_end = ATTN_WIDTH
    kv_end = q_end + 2 * KV_WIDTH
    return jnp.concatenate([a[..., :q_end], a[..., kv_end:], a[..., q_end:kv_end]], axis=-1)


def kernel(x, positions, pre_norm_g, w_in, b_in, sinks, w_dw, b_dw, conv_ln_g, conv_ln_b,
           w_pw, b_pw, w_out, b_out, post_norm_g):
    bsz, seq, d = x.shape
    depth = w_in.shape[0]
    assert d == D_MODEL and w_in.shape[2] == IN_COLS and seq % 1024 == 0
    lane = jnp.arange(LANES) % HEAD_DIM % ROTARY_HALF
    invf = (ROPE_THETA ** (-(2 * lane).astype(F32) / ROTARY_DIM)).reshape(1, LANES)

    outs = []
    for bi in range(bsz):
        xb = x[bi]
        pos = positions[bi].astype(F32).reshape(seq, 1)
        for l in range(depth):
            w_in_l = _regroup_cols(w_in[l]).astype(BF16)
            b_in_l = _regroup_cols(b_in[l]).reshape(1, IN_COLS)
            p = _in_proj(xb, pre_norm_g[l].reshape(1, d), w_in_l, b_in_l, tm=1024, tn=768)
            u = _mixers(p, pos, invf, sinks[l], w_dw[l], b_dw[l].reshape(1, -1),
                        conv_ln_g[l].reshape(1, -1), conv_ln_b[l].reshape(1, -1),
                        w_pw[l].astype(BF16), b_pw[l].reshape(1, -1), tile=256)
            xb = _out_proj(u, xb, w_out[l].astype(BF16), b_out[l].reshape(1, d),
                           post_norm_g[l].reshape(1, d), tm=512)
        outs.append(xb)
    return outs[0].reshape(1, seq, d) if bsz == 1 else jnp.stack(outs, axis=0)
```

```python
import functools

import jax
import jax.numpy as jnp
from jax import lax
from jax.experimental import pallas as pl
from jax.experimental.pallas import tpu as pltpu

D_MODEL = 2048
CONV_WIDTH = 1024
ATTN_WIDTH = 1024
HEAD_DIM = 64
N_HEADS = 16
N_KV_HEADS = 2
GQA_GROUP = N_HEADS // N_KV_HEADS
KV_WIDTH = N_KV_HEADS * HEAD_DIM
CONV_KERNEL = 31
WINDOW = 128
ROPE_THETA = 500000.0
ROTARY_DIM = HEAD_DIM // 4
ROTARY_HALF = ROTARY_DIM // 2
RMS_EPS = 1e-6
LN_EPS = 1e-5
NEG_INF = -1e30
IN_COLS = 2 * ATTN_WIDTH + 2 * KV_WIDTH + 3 * CONV_WIDTH

SECTION = 1024
SEC_Q, SEC_GATTN, SEC_VAL, SEC_GATE, SEC_GCONV = 0, 1, 2, 3, 4
KV_COL0 = 5 * SECTION

LANES = 128
SUBLANES = 8
CONV_HALO = 32
VMEM_LIMIT = 56 * 1024 * 1024

F32 = jnp.float32
BF16 = jnp.bfloat16


def _in_proj_kernel(x_ref, g_ref, w_ref, b_ref, o_ref, h_ref, *, row_chunk):
    @pl.when(pl.program_id(1) == 0)
    def _():
        def body(c, carry):
            r0 = pl.multiple_of(c * row_chunk, row_chunk)
            x = x_ref[pl.ds(r0, row_chunk), :]
            ms = jnp.mean(x * x, axis=-1, keepdims=True)
            h_ref[pl.ds(r0, row_chunk), :] = (x * lax.rsqrt(ms + RMS_EPS) * g_ref[...]).astype(BF16)
            return carry
        lax.fori_loop(0, x_ref.shape[0] // row_chunk, body, 0)

    o_ref[...] = jnp.dot(h_ref[...], w_ref[...], preferred_element_type=F32) + b_ref[...]


def _in_proj(x2d, g, w, b, *, tm, tn):
    s, d = x2d.shape
    n = w.shape[1]
    return pl.pallas_call(
        functools.partial(_in_proj_kernel, row_chunk=128),
        grid=(s // tm, n // tn),
        in_specs=[
            pl.BlockSpec((tm, d), lambda i, j: (i, 0)),
            pl.BlockSpec((1, d), lambda i, j: (0, 0)),
            pl.BlockSpec((d, tn), lambda i, j: (0, j)),
            pl.BlockSpec((1, tn), lambda i, j: (0, j)),
        ],
        out_specs=pl.BlockSpec((tm, tn), lambda i, j: (i, j)),
        out_shape=jax.ShapeDtypeStruct((s, n), F32),
        scratch_shapes=[pltpu.VMEM((tm, d), BF16)],
        compiler_params=pltpu.CompilerParams(
            dimension_semantics=("parallel", "arbitrary"), vmem_limit_bytes=VMEM_LIMIT),
        name="in_proj",
    )(x2d, g, w, b)


def _rope_table_kernel(pos_ref, invf_ref, cos_ref, sin_ref):
    ang = pos_ref[...] * invf_ref[...]
    cos_ref[...] = jnp.cos(ang)
    sin_ref[...] = jnp.sin(ang)


def _rope_tables(pos, seq):
    per_row = LANES // ROTARY_HALF
    pos_rep = jnp.repeat(pos, ROTARY_HALF).reshape(seq // per_row, LANES)
    freq = jnp.arange(LANES) % ROTARY_HALF
    invf = (ROPE_THETA ** (-(2 * freq).astype(F32) / ROTARY_DIM)).reshape(1, LANES)
    shape = jax.ShapeDtypeStruct(pos_rep.shape, F32)
    cos_c, sin_c = pl.pallas_call(_rope_table_kernel, out_shape=(shape, shape), name="rope_tables")(pos_rep, invf)
    return cos_c.reshape(seq, ROTARY_HALF), sin_c.reshape(seq, ROTARY_HALF)


def _lane_pattern(t, fill):
    seq = t.shape[0]
    head = jnp.concatenate([t, t, jnp.full((seq, HEAD_DIM - ROTARY_DIM), fill, F32)], axis=1)
    return jnp.tile(head, (1, LANES // HEAD_DIM))


def _rope(xc, cos_t, sin_t):
    d = lax.broadcasted_iota(jnp.int32, xc.shape, 1) % HEAD_DIM
    lo = d < ROTARY_HALF
    partner = jnp.where(lo, pltpu.roll(xc, LANES - ROTARY_HALF, 1),
                        pltpu.roll(xc, ROTARY_HALF, 1))
    return xc * cos_t + partner * jnp.where(lo, -sin_t, sin_t)


def _head_pad_variants(a):
    lo = lax.broadcasted_iota(jnp.int32, a.shape, 1) < HEAD_DIM
    r = pltpu.roll(a, HEAD_DIM, 1)
    return (jnp.where(lo, a, 0.0), jnp.where(lo, 0.0, r),
            jnp.where(lo, r, 0.0), jnp.where(lo, 0.0, a))


def _mixers_kernel(cos_ref, sin_ref, cosp_ref, sinp_ref, q_ref, ga_ref, val_ref, gate_ref, gc_ref, kv_ref,
                   kvp_ref, valp_ref, gatep_ref, sinks_ref, wdw_ref, bdw_ref, lng_ref, lnb_ref,
                   wpw_ref, bpw_ref, u_ref, qbuf, kbuf, vbuf, hbuf, cb, cbuf, *, tile, seg, ln_rows):
    i = pl.program_id(0)
    nblk = tile // WINDOW
    pairs = GQA_GROUP // 2
    stack = GQA_GROUP * WINDOW

    cos_t, sin_t = cos_ref[...], sin_ref[...]
    scale = HEAD_DIM ** -0.5
    for c in range(ATTN_WIDTH // LANES):
        qc = (_rope(q_ref[:, c * LANES:(c + 1) * LANES], cos_t, sin_t) * scale).astype(BF16)
        for b in range(nblk):
            dst = (b * N_KV_HEADS * pairs + c) * WINDOW
            qbuf[dst:dst + WINDOW, :] = qc[b * WINDOW:(b + 1) * WINDOW, :]
    k_prev = _head_pad_variants(_rope(kvp_ref[:, :KV_WIDTH], cosp_ref[...], sinp_ref[...]))
    k_cur = _head_pad_variants(_rope(kv_ref[:, :KV_WIDTH], cos_t, sin_t))
    v_prev = _head_pad_variants(kvp_ref[:, KV_WIDTH:])
    v_cur = _head_pad_variants(kv_ref[:, KV_WIDTH:])
    for n in range(2 * N_KV_HEADS):
        kbuf[n, :WINDOW, :] = k_prev[n].astype(BF16)
        kbuf[n, WINDOW:, :] = k_cur[n].astype(BF16)
        vbuf[n, :WINDOW, :] = v_prev[n].astype(BF16)
        vbuf[n, WINDOW:, :] = v_cur[n].astype(BF16)

    row = lax.broadcasted_iota(jnp.int32, (stack, WINDOW), 0) % WINDOW
    tri = lax.broadcasted_iota(jnp.int32, (stack, WINDOW), 1) <= row
    nt = (((1,), (1,)), ((), ()))
    for b in range(nblk):
        r0 = b * WINDOW
        for g in range(N_KV_HEADS):
            q0 = (b * N_KV_HEADS + g) * pairs * WINDOW
            q4 = qbuf[q0:q0 + pairs * WINDOW, :]
            s2 = jnp.concatenate(
                [lax.dot_general(q4, kbuf[2 * g + e, r0:r0 + 2 * WINDOW, :], nt, preferred_element_type=F32)
                 for e in range(2)], axis=0)
            s_prev = s2[:, :WINDOW]
            if b == 0:
                s_prev = jnp.where(i > 0, s_prev, NEG_INF)
            s = jnp.where(tri, s2[:, WINDOW:], s_prev)
            heads = [g * GQA_GROUP + 2 * pr + e for e in range(2) for pr in range(pairs)]
            rows_of = [slice(n * WINDOW, (n + 1) * WINDOW) for n in range(GQA_GROUP)]
            s_max = jnp.max(s, axis=-1, keepdims=True)
            m = [jnp.maximum(s_max[rs], sinks_ref[h]) for rs, h in zip(rows_of, heads)]
            ex = jnp.concatenate([jnp.exp(s[rs] - mh) for rs, mh in zip(rows_of, m)], axis=0)
            e_sum = jnp.sum(ex, axis=-1, keepdims=True)
            prob = jnp.concatenate(
                [ex[rs] * (1.0 / (e_sum[rs] + jnp.exp(sinks_ref[h] - mh)))
                 for rs, h, mh in zip(rows_of, heads, m)], axis=0)
            p2 = jnp.concatenate([jnp.where(tri, 0.0, prob), jnp.where(tri, prob, 0.0)], axis=1).astype(BF16)
            half = pairs * WINDOW
            o = (jnp.dot(p2[:half], vbuf[2 * g, r0:r0 + 2 * WINDOW, :], preferred_element_type=F32)
                 + jnp.dot(p2[half:], vbuf[2 * g + 1, r0:r0 + 2 * WINDOW, :], preferred_element_type=F32))
            for pr in range(pairs):
                c = g * pairs + pr
                ga = ga_ref[r0:r0 + WINDOW, c * LANES:(c + 1) * LANES]
                u_ref[r0:r0 + WINDOW, c * LANES:(c + 1) * LANES] = (
                    o[pr * WINDOW:(pr + 1) * WINDOW, :] * jax.nn.silu(ga)).astype(BF16)

    lead = CONV_HALO - (CONV_KERNEL - 1)
    for c in range(CONV_WIDTH // LANES):
        cs = slice(c * LANES, (c + 1) * LANES)
        hbuf[c, :CONV_HALO, :] = jnp.where(i > 0, valp_ref[:, cs] * jax.nn.sigmoid(gatep_ref[:, cs]), 0.0)
        hbuf[c, CONV_HALO:CONV_HALO + tile, :] = val_ref[:, cs] * jax.nn.sigmoid(gate_ref[:, cs])
        hbuf[c, CONV_HALO + tile:, :] = jnp.zeros((hbuf.shape[1] - CONV_HALO - tile, LANES), F32)

    unroll = 9
    for c in range(CONV_WIDTH // LANES):
        cs = slice(c * LANES, (c + 1) * LANES)
        w_taps = [jnp.broadcast_to(wdw_ref[j:j + 1, cs], (SUBLANES, LANES)) for j in range(CONV_KERNEL)]
        bias = jnp.broadcast_to(bdw_ref[:, cs], (SUBLANES, LANES))

        def conv_body(k, carry, c=c, w_taps=w_taps, bias=bias):
            for v in range(unroll):
                r = k * unroll + v
                acc = bias
                for j in range(CONV_KERNEL):
                    acc = acc + hbuf[c, pl.ds(r + lead + j, SUBLANES, stride=seg), :] * w_taps[j]
                cb[c, pl.ds(r, SUBLANES, stride=seg), :] = acc
            return carry

        lax.fori_loop(0, seg // unroll, conv_body, 0)

    def ln_body(k, carry):
        r0 = pl.multiple_of(k * ln_rows, ln_rows)
        x = jnp.concatenate([cb[c, pl.ds(r0, ln_rows), :] for c in range(CONV_WIDTH // LANES)], axis=1)
        mu = jnp.mean(x, axis=-1, keepdims=True)
        xc = x - mu
        var = jnp.mean(xc * xc, axis=-1, keepdims=True)
        y = xc * lax.rsqrt(var + LN_EPS) * lng_ref[...] + lnb_ref[...]
        cbuf[pl.ds(r0, ln_rows), :] = jax.nn.silu(y).astype(BF16)
        return carry

    lax.fori_loop(0, tile // ln_rows, ln_body, 0)

    pw = jnp.dot(cbuf[...], wpw_ref[...], preferred_element_type=F32) + bpw_ref[...]
    u_ref[:, ATTN_WIDTH:] = (pw * jax.nn.silu(gc_ref[...])).astype(BF16)


def _mixers(p, cos_t, sin_t, sinks, w_dw, b_dw, ln_g, ln_b, w_pw, b_pw, *, tile):
    s = p.shape[0]
    blk_per_tile = tile // WINDOW
    halo_per_tile = tile // CONV_HALO
    kv_blk = KV_COL0 // (2 * KV_WIDTH)
    seg = tile // SUBLANES + 4
    assert seg % 8 == 4 and SUBLANES * seg + CONV_KERNEL - 1 <= tile + 2 * CONV_HALO

    def sec(k):
        return pl.BlockSpec((tile, SECTION), lambda i, k=k: (i, k))

    def full(a):
        return pl.BlockSpec(a.shape, lambda i: (0,) * a.ndim)

    def prev_block(i):
        return jnp.maximum(i * blk_per_tile - 1, 0)

    def prev_halo(i):
        return jnp.maximum(i * halo_per_tile - 1, 0)

    cur_tab = pl.BlockSpec((tile, LANES), lambda i: (i, 0))
    prev_tab = pl.BlockSpec((WINDOW, LANES), lambda i: (prev_block(i), 0))
    return pl.pallas_call(
        functools.partial(_mixers_kernel, tile=tile, seg=seg, ln_rows=64),
        grid=(s // tile,),
        in_specs=[
            cur_tab, cur_tab, prev_tab, prev_tab,
            sec(SEC_Q), sec(SEC_GATTN), sec(SEC_VAL), sec(SEC_GATE), sec(SEC_GCONV),
            pl.BlockSpec((tile, 2 * KV_WIDTH), lambda i: (i, kv_blk)),
            pl.BlockSpec((WINDOW, 2 * KV_WIDTH), lambda i: (prev_block(i), kv_blk)),
            pl.BlockSpec((CONV_HALO, SECTION), lambda i: (prev_halo(i), SEC_VAL)),
            pl.BlockSpec((CONV_HALO, SECTION), lambda i: (prev_halo(i), SEC_GATE)),
            pl.BlockSpec(memory_space=pltpu.SMEM),
            full(w_dw), full(b_dw), full(ln_g), full(ln_b), full(w_pw), full(b_pw),
        ],
        out_specs=pl.BlockSpec((tile, D_MODEL), lambda i: (i, 0)),
        out_shape=jax.ShapeDtypeStruct((s, D_MODEL), BF16),
        scratch_shapes=[
            pltpu.VMEM((tile * ATTN_WIDTH // LANES, LANES), BF16),
            pltpu.VMEM((2 * N_KV_HEADS, WINDOW + tile, LANES), BF16),
            pltpu.VMEM((2 * N_KV_HEADS, WINDOW + tile, LANES), BF16),
            pltpu.VMEM((CONV_WIDTH // LANES, tile + 2 * CONV_HALO, LANES), F32),
            pltpu.VMEM((CONV_WIDTH // LANES, SUBLANES * seg, LANES), F32),
            pltpu.VMEM((tile, CONV_WIDTH), BF16),
        ],
        compiler_params=pltpu.CompilerParams(
            dimension_semantics=("parallel",), vmem_limit_bytes=VMEM_LIMIT),
        name="mixers",
    )(cos_t, sin_t, cos_t, sin_t, p, p, p, p, p, p, p, p, p, sinks, w_dw, b_dw, ln_g, ln_b, w_pw, b_pw)


def _out_proj_kernel(u_ref, x_ref, w_ref, b_ref, g_ref, o_ref):
    y = jnp.dot(u_ref[...], w_ref[...], preferred_element_type=F32) + b_ref[...]
    ms = jnp.mean(y * y, axis=-1, keepdims=True)
    o_ref[...] = x_ref[...] + y * lax.rsqrt(ms + RMS_EPS) * g_ref[...]


def _out_proj(u, x2d, w, b, g, *, tm):
    s, d = x2d.shape
    return pl.pallas_call(
        _out_proj_kernel,
        grid=(s // tm,),
        in_specs=[
            pl.BlockSpec((tm, u.shape[1]), lambda i: (i, 0)),
            pl.BlockSpec((tm, d), lambda i: (i, 0)),
            pl.BlockSpec(w.shape, lambda i: (0, 0)),
            pl.BlockSpec((1, d), lambda i: (0, 0)),
            pl.BlockSpec((1, d), lambda i: (0, 0)),
        ],
        out_specs=pl.BlockSpec((tm, d), lambda i: (i, 0)),
        out_shape=jax.ShapeDtypeStruct((s, d), F32),
        compiler_params=pltpu.CompilerParams(
            dimension_semantics=("parallel",), vmem_limit_bytes=VMEM_LIMIT),
        name="out_proj",
    )(u, x2d, w, b, g)


def _regroup_cols(a):
    q_end = ATTN_WIDTH
    kv_end = q_end + 2 * KV_WIDTH
    return jnp.concatenate([a[..., :q_end], a[..., kv_end:], a[..., q_end:kv_end]], axis=-1)


def kernel(x, positions, pre_norm_g, w_in, b_in, sinks, w_dw, b_dw, conv_ln_g, conv_ln_b,
           w_pw, b_pw, w_out, b_out, post_norm_g):
    bsz, seq, d = x.shape
    depth = w_in.shape[0]
    assert d == D_MODEL and w_in.shape[2] == IN_COLS and seq % 1024 == 0

    outs = []
    for bi in range(bsz):
        xb = x[bi]
        cos8, sin8 = _rope_tables(positions[bi].astype(F32), seq)
        cos_t, sin_t = _lane_pattern(cos8, 1.0), _lane_pattern(sin8, 0.0)
        for l in range(depth):
            w_in_l = _regroup_cols(w_in[l]).astype(BF16)
            b_in_l = _regroup_cols(b_in[l]).reshape(1, IN_COLS)
            p = _in_proj(xb, pre_norm_g[l].reshape(1, d), w_in_l, b_in_l, tm=1024, tn=768)
            u = _mixers(p, cos_t, sin_t, sinks[l], w_dw[l], b_dw[l].reshape(1, -1),
                        conv_ln_g[l].reshape(1, -1), conv_ln_b[l].reshape(1, -1),
                        w_pw[l].astype(BF16), b_pw[l].reshape(1, -1), tile=256)
            xb = _out_proj(u, xb, w_out[l].astype(BF16), b_out[l].reshape(1, d),
                           post_norm_g[l].reshape(1, d), tm=512)
        outs.append(xb)
    return outs[0].reshape(1, seq, d) if bsz == 1 else jnp.stack(outs, axis=0)
```

```python
import functools

import jax
import jax.numpy as jnp
from jax import lax
from jax.experimental import pallas as pl
from jax.experimental.pallas import tpu as pltpu

D_MODEL = 2048
CONV_WIDTH = 1024
ATTN_WIDTH = 1024
HEAD_DIM = 64
N_HEADS = 16
N_KV_HEADS = 2
GQA_GROUP = N_HEADS // N_KV_HEADS
KV_WIDTH = N_KV_HEADS * HEAD_DIM
CONV_KERNEL = 31
WINDOW = 128
ROPE_THETA = 500000.0
ROTARY_DIM = HEAD_DIM // 4
ROTARY_HALF = ROTARY_DIM // 2
RMS_EPS = 1e-6
LN_EPS = 1e-5
NEG_INF = -1e30
IN_COLS = 2 * ATTN_WIDTH + 2 * KV_WIDTH + 3 * CONV_WIDTH

COL_Q = 0
COL_KV = COL_Q + ATTN_WIDTH
COL_GATTN = COL_KV + 2 * KV_WIDTH
COL_VAL = COL_GATTN + ATTN_WIDTH
COL_GATE = COL_VAL + CONV_WIDTH
COL_GCONV = COL_GATE + CONV_WIDTH

LANES = 128
SUBLANES = 8
CONV_HALO = 32
VMEM_LIMIT = 56 * 1024 * 1024

F32 = jnp.float32
BF16 = jnp.bfloat16


def _in_proj_kernel(x_ref, g_ref, w_ref, b_ref, o_ref, h_ref, *, row_chunk):
    @pl.when(pl.program_id(1) == 0)
    def _():
        def body(c, carry):
            r0 = pl.multiple_of(c * row_chunk, row_chunk)
            x = x_ref[pl.ds(r0, row_chunk), :]
            ms = jnp.mean(x * x, axis=-1, keepdims=True)
            h_ref[pl.ds(r0, row_chunk), :] = (x * lax.rsqrt(ms + RMS_EPS) * g_ref[...]).astype(BF16)
            return carry
        lax.fori_loop(0, x_ref.shape[0] // row_chunk, body, 0)

    o_ref[...] = jnp.dot(h_ref[...], w_ref[...], preferred_element_type=F32) + b_ref[...]


def _in_proj(x2d, g, w, b, *, tm, tn):
    s, d = x2d.shape
    n = w.shape[1]
    return pl.pallas_call(
        functools.partial(_in_proj_kernel, row_chunk=128),
        grid=(s // tm, n // tn),
        in_specs=[
            pl.BlockSpec((tm, d), lambda i, j: (i, 0)),
            pl.BlockSpec((1, d), lambda i, j: (0, 0)),
            pl.BlockSpec((d, tn), lambda i, j: (0, j)),
            pl.BlockSpec((1, tn), lambda i, j: (0, j)),
        ],
        out_specs=pl.BlockSpec((tm, tn), lambda i, j: (i, j)),
        out_shape=jax.ShapeDtypeStruct((s, n), F32),
        scratch_shapes=[pltpu.VMEM((tm, d), BF16)],
        compiler_params=pltpu.CompilerParams(
            dimension_semantics=("parallel", "arbitrary"), vmem_limit_bytes=VMEM_LIMIT),
        name="in_proj",
    )(x2d, g, w, b)


def _rope_table_kernel(pos_ref, invf_ref, sel_ref, cos_ref, sin_ref, *, chunk):
    ang = invf_ref[...] * pos_ref[...]
    cos_a, sin_a = jnp.cos(ang), jnp.sin(ang)
    sel = sel_ref[...]
    rotary = jnp.sum(sel, axis=0, keepdims=True)
    tn = (((0,), (0,)), ((), ()))
    for k in range(chunk // LANES):
        cols = slice(k * LANES, (k + 1) * LANES)
        rows = slice(k * LANES, (k + 1) * LANES)
        cos_ref[rows, :] = lax.dot_general(cos_a[:, cols], sel, tn, precision=lax.Precision.HIGHEST,
                                           preferred_element_type=F32) + (1.0 - rotary)
        sin_ref[rows, :] = lax.dot_general(sin_a[:, cols], sel, tn, precision=lax.Precision.HIGHEST,
                                           preferred_element_type=F32)


def _rope_tables(pos, *, chunk):
    seq = pos.shape[0]
    freq = jnp.arange(ROTARY_HALF, dtype=F32)
    invf = (ROPE_THETA ** (-(2 * freq) / ROTARY_DIM)).reshape(ROTARY_HALF, 1)
    d = jnp.arange(LANES) % HEAD_DIM
    sel = ((d[None, :] % ROTARY_HALF == jnp.arange(ROTARY_HALF)[:, None]) & (d[None, :] < ROTARY_DIM)).astype(F32)
    out = jax.ShapeDtypeStruct((seq, LANES), F32)
    return pl.pallas_call(
        functools.partial(_rope_table_kernel, chunk=chunk),
        grid=(seq // chunk,),
        in_specs=[
            pl.BlockSpec((1, chunk), lambda i: (0, i)),
            pl.BlockSpec((ROTARY_HALF, 1), lambda i: (0, 0)),
            pl.BlockSpec((ROTARY_HALF, LANES), lambda i: (0, 0)),
        ],
        out_specs=[pl.BlockSpec((chunk, LANES), lambda i: (i, 0))] * 2,
        out_shape=(out, out),
        compiler_params=pltpu.CompilerParams(dimension_semantics=("parallel",)),
        name="rope_tables",
    )(pos.reshape(1, seq), invf, sel)


def _rope(xc, cos_t, sin_t):
    d = lax.broadcasted_iota(jnp.int32, xc.shape, 1) % HEAD_DIM
    lo = d < ROTARY_HALF
    partner = jnp.where(lo, pltpu.roll(xc, LANES - ROTARY_HALF, 1),
                        pltpu.roll(xc, ROTARY_HALF, 1))
    return xc * cos_t + partner * jnp.where(lo, -sin_t, sin_t)


def _head_pad_variants(a):
    lo = lax.broadcasted_iota(jnp.int32, a.shape, 1) < HEAD_DIM
    r = pltpu.roll(a, HEAD_DIM, 1)
    return (jnp.where(lo, a, 0.0), jnp.where(lo, 0.0, r),
            jnp.where(lo, r, 0.0), jnp.where(lo, 0.0, a))


def _mixers_kernel(cos_ref, sin_ref, cosp_ref, sinp_ref, q_ref, ga_ref, val_ref, gate_ref, gc_ref, kv_ref,
                   kvp_ref, valp_ref, gatep_ref, sinks_ref, wdw_ref, bdw_ref, lng_ref, lnb_ref,
                   wpw_ref, bpw_ref, u_ref, qbuf, kbuf, vbuf, hbuf, cb, cbuf, *, tile, seg, ln_rows):
    i = pl.program_id(0)
    nblk = tile // WINDOW
    pairs = GQA_GROUP // 2
    stack = GQA_GROUP * WINDOW

    cos_t, sin_t = cos_ref[...], sin_ref[...]
    scale = HEAD_DIM ** -0.5
    for c in range(ATTN_WIDTH // LANES):
        qc = (_rope(q_ref[:, c * LANES:(c + 1) * LANES], cos_t, sin_t) * scale).astype(BF16)
        for b in range(nblk):
            dst = (b * N_KV_HEADS * pairs + c) * WINDOW
            qbuf[dst:dst + WINDOW, :] = qc[b * WINDOW:(b + 1) * WINDOW, :]
    k_prev = _head_pad_variants(_rope(kvp_ref[:, :KV_WIDTH], cosp_ref[...], sinp_ref[...]))
    k_cur = _head_pad_variants(_rope(kv_ref[:, :KV_WIDTH], cos_t, sin_t))
    v_prev = _head_pad_variants(kvp_ref[:, KV_WIDTH:])
    v_cur = _head_pad_variants(kv_ref[:, KV_WIDTH:])
    for n in range(2 * N_KV_HEADS):
        kbuf[n, :WINDOW, :] = k_prev[n].astype(BF16)
        kbuf[n, WINDOW:, :] = k_cur[n].astype(BF16)
        vbuf[n, :WINDOW, :] = v_prev[n].astype(BF16)
        vbuf[n, WINDOW:, :] = v_cur[n].astype(BF16)

    row = lax.broadcasted_iota(jnp.int32, (stack, WINDOW), 0) % WINDOW
    tri = lax.broadcasted_iota(jnp.int32, (stack, WINDOW), 1) <= row
    nt = (((1,), (1,)), ((), ()))
    for b in range(nblk):
        r0 = b * WINDOW
        for g in range(N_KV_HEADS):
            q0 = (b * N_KV_HEADS + g) * pairs * WINDOW
            q4 = qbuf[q0:q0 + pairs * WINDOW, :]
            s2 = jnp.concatenate(
                [lax.dot_general(q4, kbuf[2 * g + e, r0:r0 + 2 * WINDOW, :], nt, preferred_element_type=F32)
                 for e in range(2)], axis=0)
            s_prev = s2[:, :WINDOW]
            if b == 0:
                s_prev = jnp.where(i > 0, s_prev, NEG_INF)
            s = jnp.where(tri, s2[:, WINDOW:], s_prev)
            heads = [g * GQA_GROUP + 2 * pr + e for e in range(2) for pr in range(pairs)]
            rows_of = [slice(n * WINDOW, (n + 1) * WINDOW) for n in range(GQA_GROUP)]
            s_max = jnp.max(s, axis=-1, keepdims=True)
            m = [jnp.maximum(s_max[rs], sinks_ref[h]) for rs, h in zip(rows_of, heads)]
            ex = jnp.concatenate([jnp.exp(s[rs] - mh) for rs, mh in zip(rows_of, m)], axis=0)
            e_sum = jnp.sum(ex, axis=-1, keepdims=True)
            prob = jnp.concatenate(
                [ex[rs] * (1.0 / (e_sum[rs] + jnp.exp(sinks_ref[h] - mh)))
                 for rs, h, mh in zip(rows_of, heads, m)], axis=0)
            p2 = jnp.concatenate([jnp.where(tri, 0.0, prob), jnp.where(tri, prob, 0.0)], axis=1).astype(BF16)
            half = pairs * WINDOW
            o = (jnp.dot(p2[:half], vbuf[2 * g, r0:r0 + 2 * WINDOW, :], preferred_element_type=F32)
                 + jnp.dot(p2[half:], vbuf[2 * g + 1, r0:r0 + 2 * WINDOW, :], preferred_element_type=F32))
            for pr in range(pairs):
                c = g * pairs + pr
                ga = ga_ref[r0:r0 + WINDOW, c * LANES:(c + 1) * LANES]
                u_ref[r0:r0 + WINDOW, c * LANES:(c + 1) * LANES] = (
                    o[pr * WINDOW:(pr + 1) * WINDOW, :] * jax.nn.silu(ga)).astype(BF16)

    lead = CONV_HALO - (CONV_KERNEL - 1)
    for c in range(CONV_WIDTH // LANES):
        cs = slice(c * LANES, (c + 1) * LANES)
        hbuf[c, :CONV_HALO, :] = jnp.where(i > 0, valp_ref[:, cs] * jax.nn.sigmoid(gatep_ref[:, cs]), 0.0)
        hbuf[c, CONV_HALO:CONV_HALO + tile, :] = val_ref[:, cs] * jax.nn.sigmoid(gate_ref[:, cs])
        hbuf[c, CONV_HALO + tile:, :] = jnp.zeros((hbuf.shape[1] - CONV_HALO - tile, LANES), F32)

    unroll = 9
    for c in range(CONV_WIDTH // LANES):
        cs = slice(c * LANES, (c + 1) * LANES)
        w_taps = [jnp.broadcast_to(wdw_ref[j:j + 1, cs], (SUBLANES, LANES)) for j in range(CONV_KERNEL)]
        bias = jnp.broadcast_to(bdw_ref[:, cs], (SUBLANES, LANES))

        def conv_body(k, carry, c=c, w_taps=w_taps, bias=bias):
            for v in range(unroll):
                r = k * unroll + v
                acc = bias
                for j in range(CONV_KERNEL):
                    acc = acc + hbuf[c, pl.ds(r + lead + j, SUBLANES, stride=seg), :] * w_taps[j]
                cb[c, pl.ds(r, SUBLANES, stride=seg), :] = acc
            return carry

        lax.fori_loop(0, seg // unroll, conv_body, 0)

    def ln_body(k, carry):
        r0 = pl.multiple_of(k * ln_rows, ln_rows)
        x = jnp.concatenate([cb[c, pl.ds(r0, ln_rows), :] for c in range(CONV_WIDTH // LANES)], axis=1)
        mu = jnp.mean(x, axis=-1, keepdims=True)
        xc = x - mu
        var = jnp.mean(xc * xc, axis=-1, keepdims=True)
        y = xc * lax.rsqrt(var + LN_EPS) * lng_ref[...] + lnb_ref[...]
        cbuf[pl.ds(r0, ln_rows), :] = jax.nn.silu(y).astype(BF16)
        return carry

    lax.fori_loop(0, tile // ln_rows, ln_body, 0)

    pw = jnp.dot(cbuf[...], wpw_ref[...], preferred_element_type=F32) + bpw_ref[...]
    u_ref[:, ATTN_WIDTH:] = (pw * jax.nn.silu(gc_ref[...])).astype(BF16)


def _mixers(p, cos_t, sin_t, sinks, w_dw, b_dw, ln_g, ln_b, w_pw, b_pw, *, tile):
    s = p.shape[0]
    blk_per_tile = tile // WINDOW
    halo_per_tile = tile // CONV_HALO
    seg = tile // SUBLANES + 4
    assert seg % 8 == 4 and SUBLANES * seg + CONV_KERNEL - 1 <= tile + 2 * CONV_HALO

    def cols(rows, col0, width, row_block):
        return pl.BlockSpec((pl.Element(rows), pl.Element(width)), lambda i: (row_block(i) * rows, col0))

    def sec(col0):
        return cols(tile, col0, CONV_WIDTH, lambda i: i)

    def full(a):
        return pl.BlockSpec(a.shape, lambda i: (0,) * a.ndim)

    def prev_block(i):
        return jnp.maximum(i * blk_per_tile - 1, 0)

    def prev_halo(i):
        return jnp.maximum(i * halo_per_tile - 1, 0)

    cur_tab = pl.BlockSpec((tile, LANES), lambda i: (i, 0))
    prev_tab = pl.BlockSpec((WINDOW, LANES), lambda i: (prev_block(i), 0))
    return pl.pallas_call(
        functools.partial(_mixers_kernel, tile=tile, seg=seg, ln_rows=64),
        grid=(s // tile,),
        in_specs=[
            cur_tab, cur_tab, prev_tab, prev_tab,
            sec(COL_Q), sec(COL_GATTN), sec(COL_VAL), sec(COL_GATE), sec(COL_GCONV),
            cols(tile, COL_KV, 2 * KV_WIDTH, lambda i: i),
            cols(WINDOW, COL_KV, 2 * KV_WIDTH, prev_block),
            cols(CONV_HALO, COL_VAL, CONV_WIDTH, prev_halo),
            cols(CONV_HALO, COL_GATE, CONV_WIDTH, prev_halo),
            pl.BlockSpec(memory_space=pltpu.SMEM),
            full(w_dw), full(b_dw), full(ln_g), full(ln_b), full(w_pw), full(b_pw),
        ],
        out_specs=pl.BlockSpec((tile, D_MODEL), lambda i: (i, 0)),
        out_shape=jax.ShapeDtypeStruct((s, D_MODEL), BF16),
        scratch_shapes=[
            pltpu.VMEM((tile * ATTN_WIDTH // LANES, LANES), BF16),
            pltpu.VMEM((2 * N_KV_HEADS, WINDOW + tile, LANES), BF16),
            pltpu.VMEM((2 * N_KV_HEADS, WINDOW + tile, LANES), BF16),
            pltpu.VMEM((CONV_WIDTH // LANES, tile + 2 * CONV_HALO, LANES), F32),
            pltpu.VMEM((CONV_WIDTH // LANES, SUBLANES * seg, LANES), F32),
            pltpu.VMEM((tile, CONV_WIDTH), BF16),
        ],
        compiler_params=pltpu.CompilerParams(
            dimension_semantics=("parallel",), vmem_limit_bytes=VMEM_LIMIT),
        name="mixers",
    )(cos_t, sin_t, cos_t, sin_t, p, p, p, p, p, p, p, p, p, sinks, w_dw, b_dw, ln_g, ln_b, w_pw, b_pw)


def _out_proj_kernel(u_ref, x_ref, w_ref, b_ref, g_ref, o_ref):
    y = jnp.dot(u_ref[...], w_ref[...], preferred_element_type=F32) + b_ref[...]
    ms = jnp.mean(y * y, axis=-1, keepdims=True)
    o_ref[...] = x_ref[...] + y * lax.rsqrt(ms + RMS_EPS) * g_ref[...]


def _out_proj(u, x2d, w, b, g, *, tm):
    s, d = x2d.shape
    return pl.pallas_call(
        _out_proj_kernel,
        grid=(s // tm,),
        in_specs=[
            pl.BlockSpec((tm, u.shape[1]), lambda i: (i, 0)),
            pl.BlockSpec((tm, d), lambda i: (i, 0)),
            pl.BlockSpec(w.shape, lambda i: (0, 0)),
            pl.BlockSpec((1, d), lambda i: (0, 0)),
            pl.BlockSpec((1, d), lambda i: (0, 0)),
        ],
        out_specs=pl.BlockSpec((tm, d), lambda i: (i, 0)),
        out_shape=jax.ShapeDtypeStruct((s, d), F32),
        compiler_params=pltpu.CompilerParams(
            dimension_semantics=("parallel",), vmem_limit_bytes=VMEM_LIMIT),
        name="out_proj",
    )(u, x2d, w, b, g)


def kernel(x, positions, pre_norm_g, w_in, b_in, sinks, w_dw, b_dw, conv_ln_g, conv_ln_b,
           w_pw, b_pw, w_out, b_out, post_norm_g):
    bsz, seq, d = x.shape
    depth = w_in.shape[0]
    assert d == D_MODEL and w_in.shape[2] == IN_COLS and seq % 1024 == 0

    outs = []
    for bi in range(bsz):
        xb = x[bi]
        cos_t, sin_t = _rope_tables(positions[bi].astype(F32), chunk=1024)
        for l in range(depth):
            p = _in_proj(xb, pre_norm_g[l].reshape(1, d), w_in[l].astype(BF16), b_in[l].reshape(1, IN_COLS),
                         tm=1024, tn=768)
            u = _mixers(p, cos_t, sin_t, sinks[l], w_dw[l], b_dw[l].reshape(1, -1),
                        conv_ln_g[l].reshape(1, -1), conv_ln_b[l].reshape(1, -1),
                        w_pw[l].astype(BF16), b_pw[l].reshape(1, -1), tile=256)
            xb = _out_proj(u, xb, w_out[l].astype(BF16), b_out[l].reshape(1, d),
                           post_norm_g[l].reshape(1, d), tm=512)
        outs.append(xb)
    return outs[0].reshape(1, seq, d) if bsz == 1 else jnp.stack(outs, axis=0)
```

```python
import functools

import jax
import jax.numpy as jnp
from jax import lax
from jax.experimental import pallas as pl
from jax.experimental.pallas import tpu as pltpu

D_MODEL = 2048
CONV_WIDTH = 1024
ATTN_WIDTH = 1024
HEAD_DIM = 64
N_HEADS = 16
N_KV_HEADS = 2
GQA_GROUP = N_HEADS // N_KV_HEADS
KV_WIDTH = N_KV_HEADS * HEAD_DIM
CONV_KERNEL = 31
WINDOW = 128
ROPE_THETA = 500000.0
ROTARY_DIM = HEAD_DIM // 4
ROTARY_HALF = ROTARY_DIM // 2
RMS_EPS = 1e-6
LN_EPS = 1e-5
NEG_INF = -1e30
IN_COLS = 2 * ATTN_WIDTH + 2 * KV_WIDTH + 3 * CONV_WIDTH

COL_Q = 0
COL_KV = COL_Q + ATTN_WIDTH
COL_GATTN = COL_KV + 2 * KV_WIDTH
COL_VAL = COL_GATTN + ATTN_WIDTH
COL_GATE = COL_VAL + CONV_WIDTH
COL_GCONV = COL_GATE + CONV_WIDTH

LANES = 128
SUBLANES = 8
CONV_HALO = 32
VMEM_LIMIT = 56 * 1024 * 1024

F32 = jnp.float32
BF16 = jnp.bfloat16


def _rope_table_kernel(pos_ref, invf_ref, sel_ref, cos_ref, sin_ref, *, chunk):
    ang = invf_ref[...] * pos_ref[...]
    cos_a, sin_a = jnp.cos(ang), jnp.sin(ang)
    sel = sel_ref[...]
    rotary = jnp.sum(sel, axis=0, keepdims=True)
    tn = (((0,), (0,)), ((), ()))
    for k in range(chunk // LANES):
        cols = slice(k * LANES, (k + 1) * LANES)
        rows = slice(k * LANES, (k + 1) * LANES)
        cos_ref[rows, :] = lax.dot_general(cos_a[:, cols], sel, tn, precision=lax.Precision.HIGHEST,
                                           preferred_element_type=F32) + (1.0 - rotary)
        sin_ref[rows, :] = lax.dot_general(sin_a[:, cols], sel, tn, precision=lax.Precision.HIGHEST,
                                           preferred_element_type=F32)


def _rope_tables(pos, *, chunk):
    seq = pos.shape[0]
    freq = jnp.arange(ROTARY_HALF, dtype=F32)
    invf = (ROPE_THETA ** (-(2 * freq) / ROTARY_DIM)).reshape(ROTARY_HALF, 1)
    d = jnp.arange(LANES) % HEAD_DIM
    sel = ((d[None, :] % ROTARY_HALF == jnp.arange(ROTARY_HALF)[:, None]) & (d[None, :] < ROTARY_DIM)).astype(F32)
    out = jax.ShapeDtypeStruct((seq, LANES), F32)
    return pl.pallas_call(
        functools.partial(_rope_table_kernel, chunk=chunk),
        grid=(seq // chunk,),
        in_specs=[
            pl.BlockSpec((1, chunk), lambda i: (0, i)),
            pl.BlockSpec((ROTARY_HALF, 1), lambda i: (0, 0)),
            pl.BlockSpec((ROTARY_HALF, LANES), lambda i: (0, 0)),
        ],
        out_specs=[pl.BlockSpec((chunk, LANES), lambda i: (i, 0))] * 2,
        out_shape=(out, out),
        compiler_params=pltpu.CompilerParams(dimension_semantics=("parallel",)),
        name="rope_tables",
    )(pos.reshape(1, seq), invf, sel)


def _rope(xc, cos_t, sin_t):
    d = lax.broadcasted_iota(jnp.int32, xc.shape, 1) % HEAD_DIM
    lo = d < ROTARY_HALF
    partner = jnp.where(lo, pltpu.roll(xc, LANES - ROTARY_HALF, 1),
                        pltpu.roll(xc, ROTARY_HALF, 1))
    return xc * cos_t + partner * jnp.where(lo, -sin_t, sin_t)


def _head_pad_variants(a):
    lo = lax.broadcasted_iota(jnp.int32, a.shape, 1) < HEAD_DIM
    r = pltpu.roll(a, HEAD_DIM, 1)
    return (jnp.where(lo, a, 0.0), jnp.where(lo, 0.0, r),
            jnp.where(lo, r, 0.0), jnp.where(lo, 0.0, a))


def _proj_mixers_kernel(x_ref, g_ref, win_ref, bin_ref, cos_ref, sin_ref, sinks_ref, wdw_ref, bdw_ref,
                        lng_ref, lnb_ref, wpw_ref, bpw_ref, u_ref,
                        hn, qbuf, kbuf, vbuf, ga_s, gc_s, hbuf, cb, cbuf, *, tile, seg, ln_rows, norm_rows):
    i = pl.program_id(0)
    nblk = tile // WINDOW
    pairs = GQA_GROUP // 2
    stack = GQA_GROUP * WINDOW
    lane_tiles = CONV_WIDTH // LANES

    @pl.when(i == 0)
    def _():
        kbuf[...] = jnp.zeros(kbuf.shape, BF16)
        vbuf[...] = jnp.zeros(vbuf.shape, BF16)
        hbuf[...] = jnp.zeros(hbuf.shape, F32)

    for n in range(2 * N_KV_HEADS):
        kbuf[n, :WINDOW, :] = kbuf[n, tile:, :]
        vbuf[n, :WINDOW, :] = vbuf[n, tile:, :]
    for c in range(lane_tiles):
        hbuf[c, :CONV_HALO, :] = hbuf[c, tile:tile + CONV_HALO, :]

    for r0 in range(0, tile, norm_rows):
        x = x_ref[r0:r0 + norm_rows, :]
        ms = jnp.mean(x * x, axis=-1, keepdims=True)
        hn[r0:r0 + norm_rows, :] = (x * lax.rsqrt(ms + RMS_EPS) * g_ref[...]).astype(BF16)

    def proj(col0, width):
        return (jnp.dot(hn[...], win_ref[:, col0:col0 + width], preferred_element_type=F32)
                + bin_ref[:, col0:col0 + width])

    qkv = proj(COL_Q, ATTN_WIDTH + 2 * KV_WIDTH)
    cos_t, sin_t = cos_ref[...], sin_ref[...]
    scale = HEAD_DIM ** -0.5
    for c in range(ATTN_WIDTH // LANES):
        qc = (_rope(qkv[:, c * LANES:(c + 1) * LANES], cos_t, sin_t) * scale).astype(BF16)
        for b in range(nblk):
            dst = (b * N_KV_HEADS * pairs + c) * WINDOW
            qbuf[dst:dst + WINDOW, :] = qc[b * WINDOW:(b + 1) * WINDOW, :]
    k_cur = _head_pad_variants(_rope(qkv[:, COL_KV:COL_KV + KV_WIDTH], cos_t, sin_t))
    v_cur = _head_pad_variants(qkv[:, COL_KV + KV_WIDTH:])
    for n in range(2 * N_KV_HEADS):
        kbuf[n, WINDOW:, :] = k_cur[n].astype(BF16)
        vbuf[n, WINDOW:, :] = v_cur[n].astype(BF16)

    ga_s[...] = proj(COL_GATTN, ATTN_WIDTH)
    vg = proj(COL_VAL, 2 * CONV_WIDTH)
    for c in range(lane_tiles):
        hbuf[c, CONV_HALO:CONV_HALO + tile, :] = (
            vg[:, c * LANES:(c + 1) * LANES]
            * jax.nn.sigmoid(vg[:, CONV_WIDTH + c * LANES:CONV_WIDTH + (c + 1) * LANES]))
    gc_s[...] = proj(COL_GCONV, CONV_WIDTH)

    row = lax.broadcasted_iota(jnp.int32, (stack, WINDOW), 0) % WINDOW
    tri = lax.broadcasted_iota(jnp.int32, (stack, WINDOW), 1) <= row
    nt = (((1,), (1,)), ((), ()))
    rows_of = [slice(n * WINDOW, (n + 1) * WINDOW) for n in range(GQA_GROUP)]
    for b in range(nblk):
        r0 = b * WINDOW
        for g in range(N_KV_HEADS):
            q0 = (b * N_KV_HEADS + g) * pairs * WINDOW
            q4 = qbuf[q0:q0 + pairs * WINDOW, :]
            s2 = jnp.concatenate(
                [lax.dot_general(q4, kbuf[2 * g + e, r0:r0 + 2 * WINDOW, :], nt, preferred_element_type=F32)
                 for e in range(2)], axis=0)
            s_prev = s2[:, :WINDOW]
            if b == 0:
                s_prev = jnp.where(i > 0, s_prev, NEG_INF)
            s = jnp.where(tri, s2[:, WINDOW:], s_prev)
            heads = [g * GQA_GROUP + 2 * pr + e for e in range(2) for pr in range(pairs)]
            s_max = jnp.max(s, axis=-1, keepdims=True)
            m = [jnp.maximum(s_max[rs], sinks_ref[h]) for rs, h in zip(rows_of, heads)]
            ex = jnp.concatenate([jnp.exp(s[rs] - mh) for rs, mh in zip(rows_of, m)], axis=0)
            e_sum = jnp.sum(ex, axis=-1, keepdims=True)
            prob = jnp.concatenate(
                [ex[rs] * (1.0 / (e_sum[rs] + jnp.exp(sinks_ref[h] - mh)))
                 for rs, h, mh in zip(rows_of, heads, m)], axis=0)
            p2 = jnp.concatenate([jnp.where(tri, 0.0, prob), jnp.where(tri, prob, 0.0)], axis=1).astype(BF16)
            half = pairs * WINDOW
            o = (jnp.dot(p2[:half], vbuf[2 * g, r0:r0 + 2 * WINDOW, :], preferred_element_type=F32)
                 + jnp.dot(p2[half:], vbuf[2 * g + 1, r0:r0 + 2 * WINDOW, :], preferred_element_type=F32))
            for pr in range(pairs):
                c = g * pairs + pr
                ga = ga_s[r0:r0 + WINDOW, c * LANES:(c + 1) * LANES]
                u_ref[r0:r0 + WINDOW, c * LANES:(c + 1) * LANES] = (
                    o[pr * WINDOW:(pr + 1) * WINDOW, :] * jax.nn.silu(ga)).astype(BF16)

    lead = CONV_HALO - (CONV_KERNEL - 1)
    unroll = 9
    for c in range(lane_tiles):
        cs = slice(c * LANES, (c + 1) * LANES)
        w_taps = [jnp.broadcast_to(wdw_ref[j:j + 1, cs], (SUBLANES, LANES)) for j in range(CONV_KERNEL)]
        bias = jnp.broadcast_to(bdw_ref[:, cs], (SUBLANES, LANES))

        def conv_body(k, carry, c=c, w_taps=w_taps, bias=bias):
            for v in range(unroll):
                r = k * unroll + v
                acc = bias
                for j in range(CONV_KERNEL):
                    acc = acc + hbuf[c, pl.ds(r + lead + j, SUBLANES, stride=seg), :] * w_taps[j]
                cb[c, pl.ds(r, SUBLANES, stride=seg), :] = acc
            return carry

        lax.fori_loop(0, seg // unroll, conv_body, 0)

    def ln_body(k, carry):
        r0 = pl.multiple_of(k * ln_rows, ln_rows)
        x = jnp.concatenate([cb[c, pl.ds(r0, ln_rows), :] for c in range(lane_tiles)], axis=1)
        mu = jnp.mean(x, axis=-1, keepdims=True)
        xc = x - mu
        var = jnp.mean(xc * xc, axis=-1, keepdims=True)
        y = xc * lax.rsqrt(var + LN_EPS) * lng_ref[...] + lnb_ref[...]
        cbuf[pl.ds(r0, ln_rows), :] = jax.nn.silu(y).astype(BF16)
        return carry

    lax.fori_loop(0, tile // ln_rows, ln_body, 0)

    pw = jnp.dot(cbuf[...], wpw_ref[...], preferred_element_type=F32) + bpw_ref[...]
    u_ref[:, ATTN_WIDTH:] = (pw * jax.nn.silu(gc_s[...])).astype(BF16)


def _proj_mixers(x2d, g, w_in, b_in, cos_t, sin_t, sinks, w_dw, b_dw, ln_g, ln_b, w_pw, b_pw, *, tile):
    s, d = x2d.shape
    seg = tile // SUBLANES + 4
    assert seg % 8 == 4 and seg % 9 == 0 and SUBLANES * seg + CONV_KERNEL - 1 <= tile + 2 * CONV_HALO

    def resident(a):
        return pl.BlockSpec(a.shape, lambda i: (0,) * a.ndim, pipeline_mode=pl.Buffered(1))

    rows = pl.BlockSpec((tile, LANES), lambda i: (i, 0))
    return pl.pallas_call(
        functools.partial(_proj_mixers_kernel, tile=tile, seg=seg, ln_rows=64, norm_rows=64),
        grid=(s // tile,),
        in_specs=[
            pl.BlockSpec((tile, d), lambda i: (i, 0)),
            resident(g), resident(w_in), resident(b_in),
            rows, rows,
            pl.BlockSpec(memory_space=pltpu.SMEM),
            resident(w_dw), resident(b_dw), resident(ln_g), resident(ln_b), resident(w_pw), resident(b_pw),
        ],
        out_specs=pl.BlockSpec((tile, D_MODEL), lambda i: (i, 0)),
        out_shape=jax.ShapeDtypeStruct((s, D_MODEL), BF16),
        scratch_shapes=[
            pltpu.VMEM((tile, d), BF16),
            pltpu.VMEM((tile * ATTN_WIDTH // LANES, LANES), BF16),
            pltpu.VMEM((2 * N_KV_HEADS, WINDOW + tile, LANES), BF16),
            pltpu.VMEM((2 * N_KV_HEADS, WINDOW + tile, LANES), BF16),
            pltpu.VMEM((tile, ATTN_WIDTH), F32),
            pltpu.VMEM((tile, CONV_WIDTH), F32),
            pltpu.VMEM((CONV_WIDTH // LANES, tile + 2 * CONV_HALO, LANES), F32),
            pltpu.VMEM((CONV_WIDTH // LANES, SUBLANES * seg, LANES), F32),
            pltpu.VMEM((tile, CONV_WIDTH), BF16),
        ],
        compiler_params=pltpu.CompilerParams(
            dimension_semantics=("arbitrary",), vmem_limit_bytes=VMEM_LIMIT),
        name="proj_mixers",
    )(x2d, g, w_in, b_in, cos_t, sin_t, sinks, w_dw, b_dw, ln_g, ln_b, w_pw, b_pw)


def _out_proj_kernel(u_ref, x_ref, w_ref, b_ref, g_ref, o_ref):
    y = jnp.dot(u_ref[...], w_ref[...], preferred_element_type=F32) + b_ref[...]
    ms = jnp.mean(y * y, axis=-1, keepdims=True)
    o_ref[...] = x_ref[...] + y * lax.rsqrt(ms + RMS_EPS) * g_ref[...]


def _out_proj(u, x2d, w, b, g, *, tm):
    s, d = x2d.shape
    return pl.pallas_call(
        _out_proj_kernel,
        grid=(s // tm,),
        in_specs=[
            pl.BlockSpec((tm, u.shape[1]), lambda i: (i, 0)),
            pl.BlockSpec((tm, d), lambda i: (i, 0)),
            pl.BlockSpec(w.shape, lambda i: (0, 0)),
            pl.BlockSpec((1, d), lambda i: (0, 0)),
            pl.BlockSpec((1, d), lambda i: (0, 0)),
        ],
        out_specs=pl.BlockSpec((tm, d), lambda i: (i, 0)),
        out_shape=jax.ShapeDtypeStruct((s, d), F32),
        compiler_params=pltpu.CompilerParams(
            dimension_semantics=("parallel",), vmem_limit_bytes=VMEM_LIMIT),
        name="out_proj",
    )(u, x2d, w, b, g)


def kernel(x, positions, pre_norm_g, w_in, b_in, sinks, w_dw, b_dw, conv_ln_g, conv_ln_b,
           w_pw, b_pw, w_out, b_out, post_norm_g):
    bsz, seq, d = x.shape
    depth = w_in.shape[0]
    assert d == D_MODEL and w_in.shape[2] == IN_COLS and seq % 1024 == 0

    outs = []
    for bi in range(bsz):
        xb = x[bi]
        cos_t, sin_t = _rope_tables(positions[bi].astype(F32), chunk=1024)
        for l in range(depth):
            u = _proj_mixers(xb, pre_norm_g[l].reshape(1, d), w_in[l].astype(BF16), b_in[l].reshape(1, IN_COLS),
                             cos_t, sin_t, sinks[l], w_dw[l], b_dw[l].reshape(1, -1),
                             conv_ln_g[l].reshape(1, -1), conv_ln_b[l].reshape(1, -1),
                             w_pw[l].astype(BF16), b_pw[l].reshape(1, -1), tile=256)
            xb = _out_proj(u, xb, w_out[l].astype(BF16), b_out[l].reshape(1, d),
                           post_norm_g[l].reshape(1, d), tm=512)
        outs.append(xb)
    return outs[0].reshape(1, seq, d) if bsz == 1 else jnp.stack(outs, axis=0)
```

```python
import functools

import jax
import jax.numpy as jnp
from jax import lax
from jax.experimental import pallas as pl
from jax.experimental.pallas import tpu as pltpu

D_MODEL = 2048
CONV_WIDTH = 1024
ATTN_WIDTH = 1024
HEAD_DIM = 64
N_HEADS = 16
N_KV_HEADS = 2
GQA_GROUP = N_HEADS // N_KV_HEADS
KV_WIDTH = N_KV_HEADS * HEAD_DIM
CONV_KERNEL = 31
WINDOW = 128
ROPE_THETA = 500000.0
ROTARY_DIM = HEAD_DIM // 4
ROTARY_HALF = ROTARY_DIM // 2
RMS_EPS = 1e-6
LN_EPS = 1e-5
NEG_INF = -1e30
IN_COLS = 2 * ATTN_WIDTH + 2 * KV_WIDTH + 3 * CONV_WIDTH

COL_Q = 0
COL_KV = COL_Q + ATTN_WIDTH
COL_GATTN = COL_KV + 2 * KV_WIDTH
COL_VAL = COL_GATTN + ATTN_WIDTH
COL_GATE = COL_VAL + CONV_WIDTH
COL_GCONV = COL_GATE + CONV_WIDTH

LANES = 128
SUBLANES = 8
CONV_HALO = 32
VMEM_LIMIT = 56 * 1024 * 1024

F32 = jnp.float32
BF16 = jnp.bfloat16


def _rope_table_kernel(pos_ref, invf_ref, sel_ref, cos_ref, sin_ref, *, chunk):
    ang = invf_ref[...] * pos_ref[...]
    cos_a, sin_a = jnp.cos(ang), jnp.sin(ang)
    sel = sel_ref[...]
    rotary = jnp.sum(sel, axis=0, keepdims=True)
    tn = (((0,), (0,)), ((), ()))
    for k in range(chunk // LANES):
        cols = slice(k * LANES, (k + 1) * LANES)
        rows = slice(k * LANES, (k + 1) * LANES)
        cos_ref[rows, :] = lax.dot_general(cos_a[:, cols], sel, tn, precision=lax.Precision.HIGHEST,
                                           preferred_element_type=F32) + (1.0 - rotary)
        sin_ref[rows, :] = lax.dot_general(sin_a[:, cols], sel, tn, precision=lax.Precision.HIGHEST,
                                           preferred_element_type=F32)


def _rope_tables(pos, *, chunk):
    seq = pos.shape[0]
    freq = jnp.arange(ROTARY_HALF, dtype=F32)
    invf = (ROPE_THETA ** (-(2 * freq) / ROTARY_DIM)).reshape(ROTARY_HALF, 1)
    d = jnp.arange(LANES) % HEAD_DIM
    sel = ((d[None, :] % ROTARY_HALF == jnp.arange(ROTARY_HALF)[:, None]) & (d[None, :] < ROTARY_DIM)).astype(F32)
    out = jax.ShapeDtypeStruct((seq, LANES), F32)
    return pl.pallas_call(
        functools.partial(_rope_table_kernel, chunk=chunk),
        grid=(seq // chunk,),
        in_specs=[
            pl.BlockSpec((1, chunk), lambda i: (0, i)),
            pl.BlockSpec((ROTARY_HALF, 1), lambda i: (0, 0)),
            pl.BlockSpec((ROTARY_HALF, LANES), lambda i: (0, 0)),
        ],
        out_specs=[pl.BlockSpec((chunk, LANES), lambda i: (i, 0))] * 2,
        out_shape=(out, out),
        compiler_params=pltpu.CompilerParams(dimension_semantics=("parallel",)),
        name="rope_tables",
    )(pos.reshape(1, seq), invf, sel)


def _rope(xc, cos_t, sin_t):
    d = lax.broadcasted_iota(jnp.int32, xc.shape, 1) % HEAD_DIM
    lo = d < ROTARY_HALF
    partner = jnp.where(lo, pltpu.roll(xc, LANES - ROTARY_HALF, 1),
                        pltpu.roll(xc, ROTARY_HALF, 1))
    return xc * cos_t + partner * jnp.where(lo, -sin_t, sin_t)


def _head_pad_variants(a):
    lo = lax.broadcasted_iota(jnp.int32, a.shape, 1) < HEAD_DIM
    r = pltpu.roll(a, HEAD_DIM, 1)
    return (jnp.where(lo, a, 0.0), jnp.where(lo, 0.0, r),
            jnp.where(lo, r, 0.0), jnp.where(lo, 0.0, a))


def _proj_mixers_kernel(x_ref, g_ref, win_ref, bin_ref, cos_ref, sin_ref, sinks_ref, wdw_ref, bdw_ref,
                        lng_ref, lnb_ref, wpw_ref, bpw_ref, u_ref,
                        hn, qbuf, kbuf, vbuf, ga_s, gc_s, hbuf, cb, cbuf, *, tile, seg, ln_rows, norm_rows):
    i = pl.program_id(0)
    nblk = tile // WINDOW
    pairs = GQA_GROUP // 2
    stack = GQA_GROUP * WINDOW
    lane_tiles = CONV_WIDTH // LANES

    @pl.when(i == 0)
    def _():
        kbuf[...] = jnp.zeros(kbuf.shape, BF16)
        vbuf[...] = jnp.zeros(vbuf.shape, BF16)
        hbuf[...] = jnp.zeros(hbuf.shape, F32)

    for n in range(2 * N_KV_HEADS):
        kbuf[n, :WINDOW, :] = kbuf[n, tile:, :]
        vbuf[n, :WINDOW, :] = vbuf[n, tile:, :]
    for c in range(lane_tiles):
        hbuf[c, :CONV_HALO, :] = hbuf[c, tile:tile + CONV_HALO, :]

    for r0 in range(0, tile, norm_rows):
        x = x_ref[r0:r0 + norm_rows, :]
        ms = jnp.mean(x * x, axis=-1, keepdims=True)
        hn[r0:r0 + norm_rows, :] = (x * lax.rsqrt(ms + RMS_EPS) * g_ref[...]).astype(BF16)

    def proj(col0, width):
        return (jnp.dot(hn[...], win_ref[:, col0:col0 + width], preferred_element_type=F32)
                + bin_ref[:, col0:col0 + width])

    vg = proj(COL_VAL, 2 * CONV_WIDTH)
    for c in range(lane_tiles):
        hbuf[c, CONV_HALO:CONV_HALO + tile, :] = (
            vg[:, c * LANES:(c + 1) * LANES]
            * jax.nn.sigmoid(vg[:, CONV_WIDTH + c * LANES:CONV_WIDTH + (c + 1) * LANES]))

    qkv = proj(COL_Q, ATTN_WIDTH + 2 * KV_WIDTH)
    cos_t, sin_t = cos_ref[...], sin_ref[...]
    scale = HEAD_DIM ** -0.5
    for c in range(ATTN_WIDTH // LANES):
        qc = (_rope(qkv[:, c * LANES:(c + 1) * LANES], cos_t, sin_t) * scale).astype(BF16)
        for b in range(nblk):
            dst = (b * N_KV_HEADS * pairs + c) * WINDOW
            qbuf[dst:dst + WINDOW, :] = qc[b * WINDOW:(b + 1) * WINDOW, :]
    k_cur = _head_pad_variants(_rope(qkv[:, COL_KV:COL_KV + KV_WIDTH], cos_t, sin_t))
    v_cur = _head_pad_variants(qkv[:, COL_KV + KV_WIDTH:])
    for n in range(2 * N_KV_HEADS):
        kbuf[n, WINDOW:, :] = k_cur[n].astype(BF16)
        vbuf[n, WINDOW:, :] = v_cur[n].astype(BF16)

    ga_s[...] = proj(COL_GATTN, ATTN_WIDTH)
    gc_s[...] = proj(COL_GCONV, CONV_WIDTH)

    row = lax.broadcasted_iota(jnp.int32, (stack, WINDOW), 0) % WINDOW
    tri = lax.broadcasted_iota(jnp.int32, (stack, WINDOW), 1) <= row
    nt = (((1,), (1,)), ((), ()))
    rows_of = [slice(n * WINDOW, (n + 1) * WINDOW) for n in range(GQA_GROUP)]
    for b in range(nblk):
        r0 = b * WINDOW
        for g in range(N_KV_HEADS):
            q0 = (b * N_KV_HEADS + g) * pairs * WINDOW
            q4 = qbuf[q0:q0 + pairs * WINDOW, :]
            s2 = jnp.concatenate(
                [lax.dot_general(q4, kbuf[2 * g + e, r0:r0 + 2 * WINDOW, :], nt, preferred_element_type=F32)
                 for e in range(2)], axis=0)
            s_prev = s2[:, :WINDOW]
            if b == 0:
                s_prev = jnp.where(i > 0, s_prev, NEG_INF)
            s = jnp.where(tri, s2[:, WINDOW:], s_prev)
            heads = [g * GQA_GROUP + 2 * pr + e for e in range(2) for pr in range(pairs)]
            s_max = jnp.max(s, axis=-1, keepdims=True)
            m = [jnp.maximum(s_max[rs], sinks_ref[h]) for rs, h in zip(rows_of, heads)]
            ex = jnp.concatenate([jnp.exp(s[rs] - mh) for rs, mh in zip(rows_of, m)], axis=0)
            e_sum = jnp.sum(ex, axis=-1, keepdims=True)
            prob = jnp.concatenate(
                [ex[rs] * (1.0 / (e_sum[rs] + jnp.exp(sinks_ref[h] - mh)))
                 for rs, h, mh in zip(rows_of, heads, m)], axis=0)
            p2 = jnp.concatenate([jnp.where(tri, 0.0, prob), jnp.where(tri, prob, 0.0)], axis=1).astype(BF16)
            half = pairs * WINDOW
            o = (jnp.dot(p2[:half], vbuf[2 * g, r0:r0 + 2 * WINDOW, :], preferred_element_type=F32)
                 + jnp.dot(p2[half:], vbuf[2 * g + 1, r0:r0 + 2 * WINDOW, :], preferred_element_type=F32))
            for pr in range(pairs):
                c = g * pairs + pr
                ga = ga_s[r0:r0 + WINDOW, c * LANES:(c + 1) * LANES]
                u_ref[r0:r0 + WINDOW, c * LANES:(c + 1) * LANES] = (
                    o[pr * WINDOW:(pr + 1) * WINDOW, :] * jax.nn.silu(ga)).astype(BF16)

    lead = CONV_HALO - (CONV_KERNEL - 1)
    for c in range(lane_tiles):
        cs = slice(c * LANES, (c + 1) * LANES)
        w_taps = [jnp.broadcast_to(wdw_ref[j:j + 1, cs], (SUBLANES, LANES)) for j in range(CONV_KERNEL)]
        bias = jnp.broadcast_to(bdw_ref[:, cs], (SUBLANES, LANES))
        for r in range(seg):
            acc = bias
            for j in range(CONV_KERNEL):
                acc = acc + hbuf[c, pl.ds(r + lead + j, SUBLANES, stride=seg), :] * w_taps[j]
            cb[c, pl.ds(r, SUBLANES, stride=seg), :] = acc

    for r0 in range(0, tile, ln_rows):
        x = jnp.concatenate([cb[c, r0:r0 + ln_rows, :] for c in range(lane_tiles)], axis=1)
        mu = jnp.mean(x, axis=-1, keepdims=True)
        xc = x - mu
        var = jnp.mean(xc * xc, axis=-1, keepdims=True)
        y = xc * lax.rsqrt(var + LN_EPS) * lng_ref[...] + lnb_ref[...]
        cbuf[r0:r0 + ln_rows, :] = jax.nn.silu(y).astype(BF16)

    pw = jnp.dot(cbuf[...], wpw_ref[...], preferred_element_type=F32) + bpw_ref[...]
    u_ref[:, ATTN_WIDTH:] = (pw * jax.nn.silu(gc_s[...])).astype(BF16)


def _proj_mixers(x2d, g, w_in, b_in, cos_t, sin_t, sinks, w_dw, b_dw, ln_g, ln_b, w_pw, b_pw, *, tile):
    s, d = x2d.shape
    seg = tile // SUBLANES + 4
    assert seg % 8 == 4 and SUBLANES * seg + CONV_KERNEL - 1 <= tile + 2 * CONV_HALO

    def resident(a):
        return pl.BlockSpec(a.shape, lambda i: (0,) * a.ndim, pipeline_mode=pl.Buffered(1))

    rows = pl.BlockSpec((tile, LANES), lambda i: (i, 0))
    return pl.pallas_call(
        functools.partial(_proj_mixers_kernel, tile=tile, seg=seg, ln_rows=64, norm_rows=64),
        grid=(s // tile,),
        in_specs=[
            pl.BlockSpec((tile, d), lambda i: (i, 0)),
            resident(g), resident(w_in), resident(b_in),
            rows, rows,
            pl.BlockSpec(memory_space=pltpu.SMEM),
            resident(w_dw), resident(b_dw), resident(ln_g), resident(ln_b), resident(w_pw), resident(b_pw),
        ],
        out_specs=pl.BlockSpec((tile, D_MODEL), lambda i: (i, 0)),
        out_shape=jax.ShapeDtypeStruct((s, D_MODEL), BF16),
        scratch_shapes=[
            pltpu.VMEM((tile, d), BF16),
            pltpu.VMEM((tile * ATTN_WIDTH // LANES, LANES), BF16),
            pltpu.VMEM((2 * N_KV_HEADS, WINDOW + tile, LANES), BF16),
            pltpu.VMEM((2 * N_KV_HEADS, WINDOW + tile, LANES), BF16),
            pltpu.VMEM((tile, ATTN_WIDTH), F32),
            pltpu.VMEM((tile, CONV_WIDTH), F32),
            pltpu.VMEM((CONV_WIDTH // LANES, tile + 2 * CONV_HALO, LANES), F32),
            pltpu.VMEM((CONV_WIDTH // LANES, SUBLANES * seg, LANES), F32),
            pltpu.VMEM((tile, CONV_WIDTH), BF16),
        ],
        compiler_params=pltpu.CompilerParams(
            dimension_semantics=("arbitrary",), vmem_limit_bytes=VMEM_LIMIT),
        name="proj_mixers",
    )(x2d, g, w_in, b_in, cos_t, sin_t, sinks, w_dw, b_dw, ln_g, ln_b, w_pw, b_pw)


def _out_proj_kernel(u_ref, x_ref, w_ref, b_ref, g_ref, o_ref):
    y = jnp.dot(u_ref[...], w_ref[...], preferred_element_type=F32) + b_ref[...]
    ms = jnp.mean(y * y, axis=-1, keepdims=True)
    o_ref[...] = x_ref[...] + y * lax.rsqrt(ms + RMS_EPS) * g_ref[...]


def _out_proj(u, x2d, w, b, g, *, tm):
    s, d = x2d.shape
    return pl.pallas_call(
        _out_proj_kernel,
        grid=(s // tm,),
        in_specs=[
            pl.BlockSpec((tm, u.shape[1]), lambda i: (i, 0)),
            pl.BlockSpec((tm, d), lambda i: (i, 0)),
            pl.BlockSpec(w.shape, lambda i: (0, 0)),
            pl.BlockSpec((1, d), lambda i: (0, 0)),
            pl.BlockSpec((1, d), lambda i: (0, 0)),
        ],
        out_specs=pl.BlockSpec((tm, d), lambda i: (i, 0)),
        out_shape=jax.ShapeDtypeStruct((s, d), F32),
        compiler_params=pltpu.CompilerParams(
            dimension_semantics=("parallel",), vmem_limit_bytes=VMEM_LIMIT),
        name="out_proj",
    )(u, x2d, w, b, g)


def kernel(x, positions, pre_norm_g, w_in, b_in, sinks, w_dw, b_dw, conv_ln_g, conv_ln_b,
           w_pw, b_pw, w_out, b_out, post_norm_g):
    bsz, seq, d = x.shape
    depth = w_in.shape[0]
    assert d == D_MODEL and w_in.shape[2] == IN_COLS and seq % 1024 == 0

    outs = []
    for bi in range(bsz):
        xb = x[bi]
        cos_t, sin_t = _rope_tables(positions[bi].astype(F32), chunk=1024)
        for l in range(depth):
            u = _proj_mixers(xb, pre_norm_g[l].reshape(1, d), w_in[l].astype(BF16), b_in[l].reshape(1, IN_COLS),
                             cos_t, sin_t, sinks[l], w_dw[l], b_dw[l].reshape(1, -1),
                             conv_ln_g[l].reshape(1, -1), conv_ln_b[l].reshape(1, -1),
                             w_pw[l].astype(BF16), b_pw[l].reshape(1, -1), tile=256)
            xb = _out_proj(u, xb, w_out[l].astype(BF16), b_out[l].reshape(1, d),
                           post_norm_g[l].reshape(1, d), tm=512)
        outs.append(xb)
    return outs[0].reshape(1, seq, d) if bsz == 1 else jnp.stack(outs, axis=0)
```

```python
import functools

import jax
import jax.numpy as jnp
from jax import lax
from jax.experimental import pallas as pl
from jax.experimental.pallas import tpu as pltpu

D_MODEL = 2048
CONV_WIDTH = 1024
ATTN_WIDTH = 1024
HEAD_DIM = 64
N_HEADS = 16
N_KV_HEADS = 2
GQA_GROUP = N_HEADS // N_KV_HEADS
KV_WIDTH = N_KV_HEADS * HEAD_DIM
CONV_KERNEL = 31
WINDOW = 128
ROPE_THETA = 500000.0
ROTARY_DIM = HEAD_DIM // 4
ROTARY_HALF = ROTARY_DIM // 2
RMS_EPS = 1e-6
LN_EPS = 1e-5
NEG_INF = -1e30
IN_COLS = 2 * ATTN_WIDTH + 2 * KV_WIDTH + 3 * CONV_WIDTH

COL_Q = 0
COL_KV = COL_Q + ATTN_WIDTH
COL_GATTN = COL_KV + 2 * KV_WIDTH
COL_VAL = COL_GATTN + ATTN_WIDTH
COL_GATE = COL_VAL + CONV_WIDTH
COL_GCONV = COL_GATE + CONV_WIDTH

LANES = 128
SUBLANES = 8
CONV_HALO = 32
VMEM_LIMIT = 56 * 1024 * 1024

F32 = jnp.float32
BF16 = jnp.bfloat16


def _rope_table_kernel(pos_ref, invf_ref, sel_ref, cos_ref, sin_ref, *, chunk):
    ang = invf_ref[...] * pos_ref[...]
    cos_a, sin_a = jnp.cos(ang), jnp.sin(ang)
    sel = sel_ref[...]
    rotary = jnp.sum(sel, axis=0, keepdims=True)
    tn = (((0,), (0,)), ((), ()))
    for k in range(chunk // LANES):
        cols = slice(k * LANES, (k + 1) * LANES)
        rows = slice(k * LANES, (k + 1) * LANES)
        cos_ref[rows, :] = lax.dot_general(cos_a[:, cols], sel, tn, precision=lax.Precision.HIGHEST,
                                           preferred_element_type=F32) + (1.0 - rotary)
        sin_ref[rows, :] = lax.dot_general(sin_a[:, cols], sel, tn, precision=lax.Precision.HIGHEST,
                                           preferred_element_type=F32)


def _rope_tables(pos, *, chunk):
    seq = pos.shape[0]
    freq = jnp.arange(ROTARY_HALF, dtype=F32)
    invf = (ROPE_THETA ** (-(2 * freq) / ROTARY_DIM)).reshape(ROTARY_HALF, 1)
    d = jnp.arange(LANES) % HEAD_DIM
    sel = ((d[None, :] % ROTARY_HALF == jnp.arange(ROTARY_HALF)[:, None]) & (d[None, :] < ROTARY_DIM)).astype(F32)
    out = jax.ShapeDtypeStruct((seq, LANES), F32)
    return pl.pallas_call(
        functools.partial(_rope_table_kernel, chunk=chunk),
        grid=(seq // chunk,),
        in_specs=[
            pl.BlockSpec((1, chunk), lambda i: (0, i)),
            pl.BlockSpec((ROTARY_HALF, 1), lambda i: (0, 0)),
            pl.BlockSpec((ROTARY_HALF, LANES), lambda i: (0, 0)),
        ],
        out_specs=[pl.BlockSpec((chunk, LANES), lambda i: (i, 0))] * 2,
        out_shape=(out, out),
        compiler_params=pltpu.CompilerParams(dimension_semantics=("parallel",)),
        name="rope_tables",
    )(pos.reshape(1, seq), invf, sel)


def _rope(xc, cos_t, sin_t):
    d = lax.broadcasted_iota(jnp.int32, xc.shape, 1) % HEAD_DIM
    lo = d < ROTARY_HALF
    partner = jnp.where(lo, pltpu.roll(xc, LANES - ROTARY_HALF, 1),
                        pltpu.roll(xc, ROTARY_HALF, 1))
    return xc * cos_t + partner * jnp.where(lo, -sin_t, sin_t)


def _head_pad_variants(a):
    lo = lax.broadcasted_iota(jnp.int32, a.shape, 1) < HEAD_DIM
    r = pltpu.roll(a, HEAD_DIM, 1)
    return (jnp.where(lo, a, 0.0), jnp.where(lo, 0.0, r),
            jnp.where(lo, r, 0.0), jnp.where(lo, 0.0, a))


def _layer_kernel(x_ref, xprev_ref, g_ref, win_ref, bin_ref, cos_ref, sin_ref, sinks_ref, wdw_ref, bdw_ref,
                  lng_ref, lnb_ref, wpw_ref, bpw_ref, wout_ref, bout_ref, gpost_ref, o_ref,
                  hn, qbuf, kbuf, vbuf, ga_s, gc_s, hbuf, cb, cbuf, u_new, u_prev,
                  *, tile, seg, ln_rows, norm_rows):
    i = pl.program_id(0)
    nblk = tile // WINDOW
    pairs = GQA_GROUP // 2
    stack = GQA_GROUP * WINDOW
    lane_tiles = CONV_WIDTH // LANES

    @pl.when(i == 0)
    def _():
        kbuf[...] = jnp.zeros(kbuf.shape, BF16)
        vbuf[...] = jnp.zeros(vbuf.shape, BF16)
        hbuf[...] = jnp.zeros(hbuf.shape, F32)
        u_prev[...] = jnp.zeros(u_prev.shape, BF16)

    for n in range(2 * N_KV_HEADS):
        kbuf[n, :WINDOW, :] = kbuf[n, tile:, :]
        vbuf[n, :WINDOW, :] = vbuf[n, tile:, :]
    for c in range(lane_tiles):
        hbuf[c, :CONV_HALO, :] = hbuf[c, tile:tile + CONV_HALO, :]

    for r0 in range(0, tile, norm_rows):
        x = x_ref[r0:r0 + norm_rows, :]
        ms = jnp.mean(x * x, axis=-1, keepdims=True)
        hn[r0:r0 + norm_rows, :] = (x * lax.rsqrt(ms + RMS_EPS) * g_ref[...]).astype(BF16)

    def proj(col0, width):
        return (jnp.dot(hn[...], win_ref[:, col0:col0 + width], preferred_element_type=F32)
                + bin_ref[:, col0:col0 + width])

    vg = proj(COL_VAL, 2 * CONV_WIDTH)
    for c in range(lane_tiles):
        hbuf[c, CONV_HALO:CONV_HALO + tile, :] = (
            vg[:, c * LANES:(c + 1) * LANES]
            * jax.nn.sigmoid(vg[:, CONV_WIDTH + c * LANES:CONV_WIDTH + (c + 1) * LANES]))

    qkv = proj(COL_Q, ATTN_WIDTH + 2 * KV_WIDTH)
    cos_t, sin_t = cos_ref[...], sin_ref[...]
    scale = HEAD_DIM ** -0.5
    for c in range(ATTN_WIDTH // LANES):
        qc = (_rope(qkv[:, c * LANES:(c + 1) * LANES], cos_t, sin_t) * scale).astype(BF16)
        for b in range(nblk):
            dst = (b * N_KV_HEADS * pairs + c) * WINDOW
            qbuf[dst:dst + WINDOW, :] = qc[b * WINDOW:(b + 1) * WINDOW, :]
    k_cur = _head_pad_variants(_rope(qkv[:, COL_KV:COL_KV + KV_WIDTH], cos_t, sin_t))
    v_cur = _head_pad_variants(qkv[:, COL_KV + KV_WIDTH:])
    for n in range(2 * N_KV_HEADS):
        kbuf[n, WINDOW:, :] = k_cur[n].astype(BF16)
        vbuf[n, WINDOW:, :] = v_cur[n].astype(BF16)

    ga_s[...] = proj(COL_GATTN, ATTN_WIDTH)
    gc_s[...] = proj(COL_GCONV, CONV_WIDTH)

    row = lax.broadcasted_iota(jnp.int32, (stack, WINDOW), 0) % WINDOW
    tri = lax.broadcasted_iota(jnp.int32, (stack, WINDOW), 1) <= row
    nt = (((1,), (1,)), ((), ()))
    rows_of = [slice(n * WINDOW, (n + 1) * WINDOW) for n in range(GQA_GROUP)]
    for b in range(nblk):
        r0 = b * WINDOW
        for g in range(N_KV_HEADS):
            q0 = (b * N_KV_HEADS + g) * pairs * WINDOW
            q4 = qbuf[q0:q0 + pairs * WINDOW, :]
            s2 = jnp.concatenate(
                [lax.dot_general(q4, kbuf[2 * g + e, r0:r0 + 2 * WINDOW, :], nt, preferred_element_type=F32)
                 for e in range(2)], axis=0)
            s_prev = s2[:, :WINDOW]
            if b == 0:
                s_prev = jnp.where(i > 0, s_prev, NEG_INF)
            s = jnp.where(tri, s2[:, WINDOW:], s_prev)
            heads = [g * GQA_GROUP + 2 * pr + e for e in range(2) for pr in range(pairs)]
            s_max = jnp.max(s, axis=-1, keepdims=True)
            m = [jnp.maximum(s_max[rs], sinks_ref[h]) for rs, h in zip(rows_of, heads)]
            ex = jnp.concatenate([jnp.exp(s[rs] - mh) for rs, mh in zip(rows_of, m)], axis=0)
            e_sum = jnp.sum(ex, axis=-1, keepdims=True)
            prob = jnp.concatenate(
                [ex[rs] * (1.0 / (e_sum[rs] + jnp.exp(sinks_ref[h] - mh)))
                 for rs, h, mh in zip(rows_of, heads, m)], axis=0)
            p2 = jnp.concatenate([jnp.where(tri, 0.0, prob), jnp.where(tri, prob, 0.0)], axis=1).astype(BF16)
            half = pairs * WINDOW
            o = (jnp.dot(p2[:half], vbuf[2 * g, r0:r0 + 2 * WINDOW, :], preferred_element_type=F32)
                 + jnp.dot(p2[half:], vbuf[2 * g + 1, r0:r0 + 2 * WINDOW, :], preferred_element_type=F32))
            for pr in range(pairs):
                c = g * pairs + pr
                ga = ga_s[r0:r0 + WINDOW, c * LANES:(c + 1) * LANES]
                u_new[r0:r0 + WINDOW, c * LANES:(c + 1) * LANES] = (
                    o[pr * WINDOW:(pr + 1) * WINDOW, :] * jax.nn.silu(ga)).astype(BF16)

    lead = CONV_HALO - (CONV_KERNEL - 1)
    for c in range(lane_tiles):
        cs = slice(c * LANES, (c + 1) * LANES)
        w_taps = [jnp.broadcast_to(wdw_ref[j:j + 1, cs], (SUBLANES, LANES)) for j in range(CONV_KERNEL)]
        bias = jnp.broadcast_to(bdw_ref[:, cs], (SUBLANES, LANES))
        for r in range(seg):
            acc = bias
            for j in range(CONV_KERNEL):
                acc = acc + hbuf[c, pl.ds(r + lead + j, SUBLANES, stride=seg), :] * w_taps[j]
            cb[c, pl.ds(r, SUBLANES, stride=seg), :] = acc

    for r0 in range(0, tile, ln_rows):
        x = jnp.concatenate([cb[c, r0:r0 + ln_rows, :] for c in range(lane_tiles)], axis=1)
        mu = jnp.mean(x, axis=-1, keepdims=True)
        xc = x - mu
        var = jnp.mean(xc * xc, axis=-1, keepdims=True)
        y = xc * lax.rsqrt(var + LN_EPS) * lng_ref[...] + lnb_ref[...]
        cbuf[r0:r0 + ln_rows, :] = jax.nn.silu(y).astype(BF16)

    pw = jnp.dot(cbuf[...], wpw_ref[...], preferred_element_type=F32) + bpw_ref[...]
    u_new[:, ATTN_WIDTH:] = (pw * jax.nn.silu(gc_s[...])).astype(BF16)

    y = jnp.dot(u_prev[...], wout_ref[...], preferred_element_type=F32) + bout_ref[...]
    for r0 in range(0, tile, norm_rows):
        yr = y[r0:r0 + norm_rows, :]
        ms = jnp.mean(yr * yr, axis=-1, keepdims=True)
        o_ref[r0:r0 + norm_rows, :] = xprev_ref[r0:r0 + norm_rows, :] + yr * lax.rsqrt(ms + RMS_EPS) * gpost_ref[...]

    u_prev[...] = u_new[...]


def _layer(x2d, g, w_in, b_in, cos_t, sin_t, sinks, w_dw, b_dw, ln_g, ln_b, w_pw, b_pw, w_out, b_out, g_post,
           *, tile):
    s, d = x2d.shape
    n = s // tile
    seg = tile // SUBLANES + 4
    assert seg % 8 == 4 and SUBLANES * seg + CONV_KERNEL - 1 <= tile + 2 * CONV_HALO

    def resident(a):
        return pl.BlockSpec(a.shape, lambda i: (0,) * a.ndim, pipeline_mode=pl.Buffered(1))

    def cur(i):
        return jnp.minimum(i, n - 1)

    def prev(i):
        return jnp.maximum(i - 1, 0)

    rows = pl.BlockSpec((tile, LANES), lambda i: (cur(i), 0))
    return pl.pallas_call(
        functools.partial(_layer_kernel, tile=tile, seg=seg, ln_rows=64, norm_rows=64),
        grid=(n + 1,),
        in_specs=[
            pl.BlockSpec((tile, d), lambda i: (cur(i), 0)),
            pl.BlockSpec((tile, d), lambda i: (prev(i), 0)),
            resident(g), resident(w_in), resident(b_in),
            rows, rows,
            pl.BlockSpec(memory_space=pltpu.SMEM),
            resident(w_dw), resident(b_dw), resident(ln_g), resident(ln_b), resident(w_pw), resident(b_pw),
            resident(w_out), resident(b_out), resident(g_post),
        ],
        out_specs=pl.BlockSpec((tile, d), lambda i: (prev(i), 0)),
        out_shape=jax.ShapeDtypeStruct((s, d), F32),
        scratch_shapes=[
            pltpu.VMEM((tile, d), BF16),
            pltpu.VMEM((tile * ATTN_WIDTH // LANES, LANES), BF16),
            pltpu.VMEM((2 * N_KV_HEADS, WINDOW + tile, LANES), BF16),
            pltpu.VMEM((2 * N_KV_HEADS, WINDOW + tile, LANES), BF16),
            pltpu.VMEM((tile, ATTN_WIDTH), F32),
            pltpu.VMEM((tile, CONV_WIDTH), F32),
            pltpu.VMEM((CONV_WIDTH // LANES, tile + 2 * CONV_HALO, LANES), F32),
            pltpu.VMEM((CONV_WIDTH // LANES, SUBLANES * seg, LANES), F32),
            pltpu.VMEM((tile, CONV_WIDTH), BF16),
            pltpu.VMEM((tile, d), BF16),
            pltpu.VMEM((tile, d), BF16),
        ],
        compiler_params=pltpu.CompilerParams(
            dimension_semantics=("arbitrary",), vmem_limit_bytes=VMEM_LIMIT),
        name="layer",
    )(x2d, x2d, g, w_in, b_in, cos_t, sin_t, sinks, w_dw, b_dw, ln_g, ln_b, w_pw, b_pw, w_out, b_out, g_post)


def kernel(x, positions, pre_norm_g, w_in, b_in, sinks, w_dw, b_dw, conv_ln_g, conv_ln_b,
           w_pw, b_pw, w_out, b_out, post_norm_g):
    bsz, seq, d = x.shape
    depth = w_in.shape[0]
    assert d == D_MODEL and w_in.shape[2] == IN_COLS and seq % 1024 == 0

    outs = []
    for bi in range(bsz):
        xb = x[bi]
        cos_t, sin_t = _rope_tables(positions[bi].astype(F32), chunk=1024)
        for l in range(depth):
            xb = _layer(xb, pre_norm_g[l].reshape(1, d), w_in[l].astype(BF16), b_in[l].reshape(1, IN_COLS),
                        cos_t, sin_t, sinks[l], w_dw[l], b_dw[l].reshape(1, -1),
                        conv_ln_g[l].reshape(1, -1), conv_ln_b[l].reshape(1, -1),
                        w_pw[l].astype(BF16), b_pw[l].reshape(1, -1),
                        w_out[l].astype(BF16), b_out[l].reshape(1, d), post_norm_g[l].reshape(1, d), tile=256)
        outs.append(xb)
    return outs[0].reshape(1, seq, d) if bsz == 1 else jnp.stack(outs, axis=0)
```

```python
import functools

import jax
import jax.numpy as jnp
from jax import lax
from jax.experimental import pallas as pl
from jax.experimental.pallas import tpu as pltpu

D_MODEL = 2048
CONV_WIDTH = 1024
ATTN_WIDTH = 1024
HEAD_DIM = 64
N_HEADS = 16
N_KV_HEADS = 2
GQA_GROUP = N_HEADS // N_KV_HEADS
KV_WIDTH = N_KV_HEADS * HEAD_DIM
CONV_KERNEL = 31
WINDOW = 128
ROPE_THETA = 500000.0
ROTARY_DIM = HEAD_DIM // 4
ROTARY_HALF = ROTARY_DIM // 2
RMS_EPS = 1e-6
LN_EPS = 1e-5
NEG_INF = -1e30
IN_COLS = 2 * ATTN_WIDTH + 2 * KV_WIDTH + 3 * CONV_WIDTH

COL_Q = 0
COL_KV = COL_Q + ATTN_WIDTH
COL_GATTN = COL_KV + 2 * KV_WIDTH
COL_VAL = COL_GATTN + ATTN_WIDTH
COL_GATE = COL_VAL + CONV_WIDTH
COL_GCONV = COL_GATE + CONV_WIDTH

LANES = 128
SUBLANES = 8
CONV_HALO = 32
VMEM_LIMIT = 56 * 1024 * 1024

F32 = jnp.float32
BF16 = jnp.bfloat16


def _rope_table_kernel(pos_ref, invf_ref, sel_ref, cos_ref, sin_ref, *, chunk):
    ang = invf_ref[...] * pos_ref[...]
    cos_a, sin_a = jnp.cos(ang), jnp.sin(ang)
    sel = sel_ref[...]
    rotary = jnp.sum(sel, axis=0, keepdims=True)
    tn = (((0,), (0,)), ((), ()))
    for k in range(chunk // LANES):
        cols = slice(k * LANES, (k + 1) * LANES)
        rows = slice(k * LANES, (k + 1) * LANES)
        cos_ref[rows, :] = lax.dot_general(cos_a[:, cols], sel, tn, precision=lax.Precision.HIGHEST,
                                           preferred_element_type=F32) + (1.0 - rotary)
        sin_ref[rows, :] = lax.dot_general(sin_a[:, cols], sel, tn, precision=lax.Precision.HIGHEST,
                                           preferred_element_type=F32)


def _rope_tables(pos, *, chunk):
    seq = pos.shape[0]
    freq = jnp.arange(ROTARY_HALF, dtype=F32)
    invf = (ROPE_THETA ** (-(2 * freq) / ROTARY_DIM)).reshape(ROTARY_HALF, 1)
    d = jnp.arange(LANES) % HEAD_DIM
    sel = ((d[None, :] % ROTARY_HALF == jnp.arange(ROTARY_HALF)[:, None]) & (d[None, :] < ROTARY_DIM)).astype(F32)
    out = jax.ShapeDtypeStruct((seq, LANES), F32)
    return pl.pallas_call(
        functools.partial(_rope_table_kernel, chunk=chunk),
        grid=(seq // chunk,),
        in_specs=[
            pl.BlockSpec((1, chunk), lambda i: (0, i)),
            pl.BlockSpec((ROTARY_HALF, 1), lambda i: (0, 0)),
            pl.BlockSpec((ROTARY_HALF, LANES), lambda i: (0, 0)),
        ],
        out_specs=[pl.BlockSpec((chunk, LANES), lambda i: (i, 0))] * 2,
        out_shape=(out, out),
        compiler_params=pltpu.CompilerParams(dimension_semantics=("parallel",)),
        name="rope_tables",
    )(pos.reshape(1, seq), invf, sel)


def _rope(xc, cos_t, sin_t):
    d = lax.broadcasted_iota(jnp.int32, xc.shape, 1) % HEAD_DIM
    lo = d < ROTARY_HALF
    partner = jnp.where(lo, pltpu.roll(xc, LANES - ROTARY_HALF, 1),
                        pltpu.roll(xc, ROTARY_HALF, 1))
    return xc * cos_t + partner * jnp.where(lo, -sin_t, sin_t)


def _head_pad_variants(a):
    lo = lax.broadcasted_iota(jnp.int32, a.shape, 1) < HEAD_DIM
    r = pltpu.roll(a, HEAD_DIM, 1)
    return (jnp.where(lo, a, 0.0), jnp.where(lo, 0.0, r),
            jnp.where(lo, r, 0.0), jnp.where(lo, 0.0, a))


def _proj_mixers_kernel(x_ref, g_ref, win_ref, bin_ref, cos_ref, sin_ref, sinks_ref, wdw_ref, bdw_ref,
                        lng_ref, lnb_ref, wpw_ref, bpw_ref, u_ref,
                        hn, qbuf, kbuf, vbuf, ga_s, gc_s, hbuf, cb, cbuf, *, tile, seg, ln_rows, norm_rows):
    i = pl.program_id(0)
    nblk = tile // WINDOW
    pairs = GQA_GROUP // 2
    stack = GQA_GROUP * WINDOW
    lane_tiles = CONV_WIDTH // LANES

    @pl.when(i == 0)
    def _():
        kbuf[...] = jnp.zeros(kbuf.shape, BF16)
        hbuf[...] = jnp.zeros(hbuf.shape, F32)
        lane = lax.broadcasted_iota(jnp.int32, (WINDOW + tile, LANES), 1)
        for n in range(2 * N_KV_HEADS):
            vbuf[n, :, :LANES] = jnp.zeros((WINDOW + tile, LANES), BF16)
            vbuf[n, :, LANES:] = ((lane < HEAD_DIM) == (n % 2 == 0)).astype(BF16)

    for n in range(2 * N_KV_HEADS):
        kbuf[n, :WINDOW, :] = kbuf[n, tile:, :]
        vbuf[n, :WINDOW, :LANES] = vbuf[n, tile:, :LANES]
    for c in range(lane_tiles):
        hbuf[c, :CONV_HALO, :] = hbuf[c, tile:tile + CONV_HALO, :]

    for r0 in range(0, tile, norm_rows):
        x = x_ref[r0:r0 + norm_rows, :]
        ms = jnp.mean(x * x, axis=-1, keepdims=True)
        hn[r0:r0 + norm_rows, :] = (x * lax.rsqrt(ms + RMS_EPS) * g_ref[...]).astype(BF16)

    def proj(col0, width):
        return (jnp.dot(hn[...], win_ref[:, col0:col0 + width], preferred_element_type=F32)
                + bin_ref[:, col0:col0 + width])

    vg = proj(COL_VAL, 2 * CONV_WIDTH)
    for c in range(lane_tiles):
        hbuf[c, CONV_HALO:CONV_HALO + tile, :] = (
            vg[:, c * LANES:(c + 1) * LANES]
            * jax.nn.sigmoid(vg[:, CONV_WIDTH + c * LANES:CONV_WIDTH + (c + 1) * LANES]))

    qkv = proj(COL_Q, ATTN_WIDTH + 2 * KV_WIDTH)
    cos_t, sin_t = cos_ref[...], sin_ref[...]
    scale = HEAD_DIM ** -0.5
    for c in range(ATTN_WIDTH // LANES):
        qc = (_rope(qkv[:, c * LANES:(c + 1) * LANES], cos_t, sin_t) * scale).astype(BF16)
        for b in range(nblk):
            dst = (b * N_KV_HEADS * pairs + c) * WINDOW
            qbuf[dst:dst + WINDOW, :] = qc[b * WINDOW:(b + 1) * WINDOW, :]
    k_cur = _head_pad_variants(_rope(qkv[:, COL_KV:COL_KV + KV_WIDTH], cos_t, sin_t))
    v_cur = _head_pad_variants(qkv[:, COL_KV + KV_WIDTH:])
    for n in range(2 * N_KV_HEADS):
        kbuf[n, WINDOW:, :] = k_cur[n].astype(BF16)
        vbuf[n, WINDOW:, :LANES] = v_cur[n].astype(BF16)

    ga_s[...] = jax.nn.silu(proj(COL_GATTN, ATTN_WIDTH))
    gc_s[...] = proj(COL_GCONV, CONV_WIDTH)

    row = lax.broadcasted_iota(jnp.int32, (stack, WINDOW), 0) % WINDOW
    tri = lax.broadcasted_iota(jnp.int32, (stack, WINDOW), 1) <= row
    nt = (((1,), (1,)), ((), ()))
    rows_of = [slice(n * WINDOW, (n + 1) * WINDOW) for n in range(GQA_GROUP)]
    even_lanes = lax.broadcasted_iota(jnp.int32, (WINDOW, LANES), 1) < HEAD_DIM
    for b in range(nblk):
        r0 = b * WINDOW
        for g in range(N_KV_HEADS):
            q0 = (b * N_KV_HEADS + g) * pairs * WINDOW
            q4 = qbuf[q0:q0 + pairs * WINDOW, :]
            s2 = jnp.concatenate(
                [lax.dot_general(q4, kbuf[2 * g + e, r0:r0 + 2 * WINDOW, :], nt, preferred_element_type=F32)
                 for e in range(2)], axis=0)
            s_prev = s2[:, :WINDOW]
            if b == 0:
                s_prev = jnp.where(i > 0, s_prev, NEG_INF)
            s = jnp.where(tri, s2[:, WINDOW:], s_prev)
            heads = [g * GQA_GROUP + 2 * pr + e for e in range(2) for pr in range(pairs)]
            s_max = jnp.max(s, axis=-1, keepdims=True)
            m = [jnp.maximum(s_max[rs], sinks_ref[h]) for rs, h in zip(rows_of, heads)]
            ex = jnp.concatenate([jnp.exp(s[rs] - mh) for rs, mh in zip(rows_of, m)], axis=0)
            p2 = jnp.concatenate([jnp.where(tri, 0.0, ex), jnp.where(tri, ex, 0.0)], axis=1).astype(BF16)
            half = pairs * WINDOW
            o = (jnp.dot(p2[:half], vbuf[2 * g, r0:r0 + 2 * WINDOW, :], preferred_element_type=F32)
                 + jnp.dot(p2[half:], vbuf[2 * g + 1, r0:r0 + 2 * WINDOW, :], preferred_element_type=F32))
            sink_term = [jnp.exp(sinks_ref[h] - mh) for h, mh in zip(heads, m)]
            for pr in range(pairs):
                c = g * pairs + pr
                rs = rows_of[pr]
                denom = o[rs, LANES:] + jnp.where(even_lanes, sink_term[pr], sink_term[pairs + pr])
                ga = ga_s[r0:r0 + WINDOW, c * LANES:(c + 1) * LANES]
                u_ref[r0:r0 + WINDOW, c * LANES:(c + 1) * LANES] = (
                    o[rs, :LANES] * (1.0 / denom) * ga).astype(BF16)

    lead = CONV_HALO - (CONV_KERNEL - 1)
    for c in range(lane_tiles):
        cs = slice(c * LANES, (c + 1) * LANES)
        w_taps = [jnp.broadcast_to(wdw_ref[j:j + 1, cs], (SUBLANES, LANES)) for j in range(CONV_KERNEL)]
        bias = jnp.broadcast_to(bdw_ref[:, cs], (SUBLANES, LANES))
        for r in range(seg):
            acc = bias
            for j in range(CONV_KERNEL):
                acc = acc + hbuf[c, pl.ds(r + lead + j, SUBLANES, stride=seg), :] * w_taps[j]
            cb[c, pl.ds(r, SUBLANES, stride=seg), :] = acc

    for r0 in range(0, tile, ln_rows):
        x = jnp.concatenate([cb[c, r0:r0 + ln_rows, :] for c in range(lane_tiles)], axis=1)
        mu = jnp.mean(x, axis=-1, keepdims=True)
        xc = x - mu
        var = jnp.mean(xc * xc, axis=-1, keepdims=True)
        y = xc * lax.rsqrt(var + LN_EPS) * lng_ref[...] + lnb_ref[...]
        cbuf[r0:r0 + ln_rows, :] = jax.nn.silu(y).astype(BF16)

    pw = jnp.dot(cbuf[...], wpw_ref[...], preferred_element_type=F32) + bpw_ref[...]
    u_ref[:, ATTN_WIDTH:] = (pw * jax.nn.silu(gc_s[...])).astype(BF16)


def _proj_mixers(x2d, g, w_in, b_in, cos_t, sin_t, sinks, w_dw, b_dw, ln_g, ln_b, w_pw, b_pw, *, tile):
    s, d = x2d.shape
    seg = tile // SUBLANES + 4
    assert seg % 8 == 4 and SUBLANES * seg + CONV_KERNEL - 1 <= tile + 2 * CONV_HALO

    def resident(a):
        return pl.BlockSpec(a.shape, lambda i: (0,) * a.ndim, pipeline_mode=pl.Buffered(1))

    rows = pl.BlockSpec((tile, LANES), lambda i: (i, 0))
    return pl.pallas_call(
        functools.partial(_proj_mixers_kernel, tile=tile, seg=seg, ln_rows=64, norm_rows=64),
        grid=(s // tile,),
        in_specs=[
            pl.BlockSpec((tile, d), lambda i: (i, 0)),
            resident(g), resident(w_in), resident(b_in),
            rows, rows,
            pl.BlockSpec(memory_space=pltpu.SMEM),
            resident(w_dw), resident(b_dw), resident(ln_g), resident(ln_b), resident(w_pw), resident(b_pw),
        ],
        out_specs=pl.BlockSpec((tile, D_MODEL), lambda i: (i, 0)),
        out_shape=jax.ShapeDtypeStruct((s, D_MODEL), BF16),
        scratch_shapes=[
            pltpu.VMEM((tile, d), BF16),
            pltpu.VMEM((tile * ATTN_WIDTH // LANES, LANES), BF16),
            pltpu.VMEM((2 * N_KV_HEADS, WINDOW + tile, LANES), BF16),
            pltpu.VMEM((2 * N_KV_HEADS, WINDOW + tile, 2 * LANES), BF16),
            pltpu.VMEM((tile, ATTN_WIDTH), F32),
            pltpu.VMEM((tile, CONV_WIDTH), F32),
            pltpu.VMEM((CONV_WIDTH // LANES, tile + 2 * CONV_HALO, LANES), F32),
            pltpu.VMEM((CONV_WIDTH // LANES, SUBLANES * seg, LANES), F32),
            pltpu.VMEM((tile, CONV_WIDTH), BF16),
        ],
        compiler_params=pltpu.CompilerParams(
            dimension_semantics=("arbitrary",), vmem_limit_bytes=VMEM_LIMIT),
        name="proj_mixers",
    )(x2d, g, w_in, b_in, cos_t, sin_t, sinks, w_dw, b_dw, ln_g, ln_b, w_pw, b_pw)


def _out_proj_kernel(u_ref, x_ref, w_ref, b_ref, g_ref, o_ref):
    y = jnp.dot(u_ref[...], w_ref[...], preferred_element_type=F32) + b_ref[...]
    ms = jnp.mean(y * y, axis=-1, keepdims=True)
    o_ref[...] = x_ref[...] + y * lax.rsqrt(ms + RMS_EPS) * g_ref[...]


def _out_proj(u, x2d, w, b, g, *, tm):
    s, d = x2d.shape
    return pl.pallas_call(
        _out_proj_kernel,
        grid=(s // tm,),
        in_specs=[
            pl.BlockSpec((tm, u.shape[1]), lambda i: (i, 0)),
            pl.BlockSpec((tm, d), lambda i: (i, 0)),
            pl.BlockSpec(w.shape, lambda i: (0, 0)),
            pl.BlockSpec((1, d), lambda i: (0, 0)),
            pl.BlockSpec((1, d), lambda i: (0, 0)),
        ],
        out_specs=pl.BlockSpec((tm, d), lambda i: (i, 0)),
        out_shape=jax.ShapeDtypeStruct((s, d), F32),
        compiler_params=pltpu.CompilerParams(
            dimension_semantics=("parallel",), vmem_limit_bytes=VMEM_LIMIT),
        name="out_proj",
    )(u, x2d, w, b, g)


def kernel(x, positions, pre_norm_g, w_in, b_in, sinks, w_dw, b_dw, conv_ln_g, conv_ln_b,
           w_pw, b_pw, w_out, b_out, post_norm_g):
    bsz, seq, d = x.shape
    depth = w_in.shape[0]
    assert d == D_MODEL and w_in.shape[2] == IN_COLS and seq % 1024 == 0

    outs = []
    for bi in range(bsz):
        xb = x[bi]
        cos_t, sin_t = _rope_tables(positions[bi].astype(F32), chunk=1024)
        for l in range(depth):
            u = _proj_mixers(xb, pre_norm_g[l].reshape(1, d), w_in[l].astype(BF16), b_in[l].reshape(1, IN_COLS),
                             cos_t, sin_t, sinks[l], w_dw[l], b_dw[l].reshape(1, -1),
                             conv_ln_g[l].reshape(1, -1), conv_ln_b[l].reshape(1, -1),
                             w_pw[l].astype(BF16), b_pw[l].reshape(1, -1), tile=256)
            xb = _out_proj(u, xb, w_out[l].astype(BF16), b_out[l].reshape(1, d),
                           post_norm_g[l].reshape(1, d), tm=512)
        outs.append(xb)
    return outs[0].reshape(1, seq, d) if bsz == 1 else jnp.stack(outs, axis=0)
```

```python
import functools

import jax
import jax.numpy as jnp
from jax import lax
from jax.experimental import pallas as pl
from jax.experimental.pallas import tpu as pltpu

D_MODEL = 2048
CONV_WIDTH = 1024
ATTN_WIDTH = 1024
HEAD_DIM = 64
N_HEADS = 16
N_KV_HEADS = 2
GQA_GROUP = N_HEADS // N_KV_HEADS
KV_WIDTH = N_KV_HEADS * HEAD_DIM
CONV_KERNEL = 31
WINDOW = 128
ROPE_THETA = 500000.0
ROTARY_DIM = HEAD_DIM // 4
ROTARY_HALF = ROTARY_DIM // 2
RMS_EPS = 1e-6
LN_EPS = 1e-5
NEG_INF = -1e30
IN_COLS = 2 * ATTN_WIDTH + 2 * KV_WIDTH + 3 * CONV_WIDTH

COL_Q = 0
COL_KV = COL_Q + ATTN_WIDTH
COL_GATTN = COL_KV + 2 * KV_WIDTH
COL_VAL = COL_GATTN + ATTN_WIDTH
COL_GATE = COL_VAL + CONV_WIDTH
COL_GCONV = COL_GATE + CONV_WIDTH

LANES = 128
SUBLANES = 8
MXU_COLS = 256
CONV_HALO = 32
VMEM_LIMIT = 56 * 1024 * 1024

F32 = jnp.float32
BF16 = jnp.bfloat16


def _rope_table_kernel(pos_ref, invf_ref, sel_ref, cos_ref, sin_ref, *, chunk):
    ang = invf_ref[...] * pos_ref[...]
    cos_a, sin_a = jnp.cos(ang), jnp.sin(ang)
    sel = sel_ref[...]
    rotary = jnp.sum(sel, axis=0, keepdims=True)
    tn = (((0,), (0,)), ((), ()))
    for k in range(chunk // LANES):
        cols = slice(k * LANES, (k + 1) * LANES)
        rows = slice(k * LANES, (k + 1) * LANES)
        cos_ref[rows, :] = lax.dot_general(cos_a[:, cols], sel, tn, precision=lax.Precision.HIGHEST,
                                           preferred_element_type=F32) + (1.0 - rotary)
        sin_ref[rows, :] = lax.dot_general(sin_a[:, cols], sel, tn, precision=lax.Precision.HIGHEST,
                                           preferred_element_type=F32)


def _rope_tables(pos, *, chunk):
    seq = pos.shape[0]
    freq = jnp.arange(ROTARY_HALF, dtype=F32)
    invf = (ROPE_THETA ** (-(2 * freq) / ROTARY_DIM)).reshape(ROTARY_HALF, 1)
    d = jnp.arange(LANES) % HEAD_DIM
    sel = ((d[None, :] % ROTARY_HALF == jnp.arange(ROTARY_HALF)[:, None]) & (d[None, :] < ROTARY_DIM)).astype(F32)
    out = jax.ShapeDtypeStruct((seq, LANES), F32)
    return pl.pallas_call(
        functools.partial(_rope_table_kernel, chunk=chunk),
        grid=(seq // chunk,),
        in_specs=[
            pl.BlockSpec((1, chunk), lambda i: (0, i)),
            pl.BlockSpec((ROTARY_HALF, 1), lambda i: (0, 0)),
            pl.BlockSpec((ROTARY_HALF, LANES), lambda i: (0, 0)),
        ],
        out_specs=[pl.BlockSpec((chunk, LANES), lambda i: (i, 0))] * 2,
        out_shape=(out, out),
        compiler_params=pltpu.CompilerParams(dimension_semantics=("parallel",)),
        name="rope_tables",
    )(pos.reshape(1, seq), invf, sel)


def _rope(xc, cos_t, sin_t):
    d = lax.broadcasted_iota(jnp.int32, xc.shape, 1) % HEAD_DIM
    lo = d < ROTARY_HALF
    partner = jnp.where(lo, pltpu.roll(xc, LANES - ROTARY_HALF, 1),
                        pltpu.roll(xc, ROTARY_HALF, 1))
    return xc * cos_t + partner * jnp.where(lo, -sin_t, sin_t)


def _head_pad_variants(a):
    lo = lax.broadcasted_iota(jnp.int32, a.shape, 1) < HEAD_DIM
    r = pltpu.roll(a, HEAD_DIM, 1)
    return (jnp.where(lo, a, 0.0), jnp.where(lo, 0.0, r),
            jnp.where(lo, r, 0.0), jnp.where(lo, 0.0, a))


def _proj_mixers_kernel(x_ref, g_ref, win_ref, bin_ref, cos_ref, sin_ref, sinks_ref, wdw_ref, bdw_ref,
                        lng_ref, lnb_ref, wpw_ref, bpw_ref, u_ref,
                        hn, qbuf, kbuf, vbuf, ga_s, gc_s, hbuf, cb, cbuf, *, tile, seg, ln_rows, norm_rows):
    i = pl.program_id(0)
    nblk = tile // WINDOW
    pairs = GQA_GROUP // 2
    stack = GQA_GROUP * WINDOW
    lane_tiles = CONV_WIDTH // LANES

    @pl.when(i == 0)
    def _():
        kbuf[...] = jnp.zeros(kbuf.shape, BF16)
        hbuf[...] = jnp.zeros(hbuf.shape, F32)
        lane = lax.broadcasted_iota(jnp.int32, (WINDOW + tile, LANES), 1)
        for n in range(2 * N_KV_HEADS):
            vbuf[n, :, :LANES] = jnp.zeros((WINDOW + tile, LANES), BF16)
            vbuf[n, :, LANES:] = ((lane < HEAD_DIM) == (n % 2 == 0)).astype(BF16)

    for n in range(2 * N_KV_HEADS):
        kbuf[n, :WINDOW, :] = kbuf[n, tile:, :]
        vbuf[n, :WINDOW, :LANES] = vbuf[n, tile:, :LANES]
    for c in range(lane_tiles):
        hbuf[c, :CONV_HALO, :] = hbuf[c, tile:tile + CONV_HALO, :]

    for r0 in range(0, tile, norm_rows):
        x = x_ref[r0:r0 + norm_rows, :]
        ms = jnp.mean(x * x, axis=-1, keepdims=True)
        hn[r0:r0 + norm_rows, :] = (x * lax.rsqrt(ms + RMS_EPS) * g_ref[...]).astype(BF16)

    def proj(col0, width):
        return (jnp.dot(hn[...], win_ref[:, col0:col0 + width], preferred_element_type=F32)
                + bin_ref[:, col0:col0 + width])

    for k in range(CONV_WIDTH // MXU_COLS):
        val = proj(COL_VAL + k * MXU_COLS, MXU_COLS)
        gate = proj(COL_GATE + k * MXU_COLS, MXU_COLS)
        for t in range(MXU_COLS // LANES):
            hbuf[k * (MXU_COLS // LANES) + t, CONV_HALO:CONV_HALO + tile, :] = (
                val[:, t * LANES:(t + 1) * LANES] * jax.nn.sigmoid(gate[:, t * LANES:(t + 1) * LANES]))

    qkv = proj(COL_Q, ATTN_WIDTH + 2 * KV_WIDTH)
    cos_t, sin_t = cos_ref[...], sin_ref[...]
    scale = HEAD_DIM ** -0.5
    for c in range(ATTN_WIDTH // LANES):
        qc = (_rope(qkv[:, c * LANES:(c + 1) * LANES], cos_t, sin_t) * scale).astype(BF16)
        for b in range(nblk):
            dst = (b * N_KV_HEADS * pairs + c) * WINDOW
            qbuf[dst:dst + WINDOW, :] = qc[b * WINDOW:(b + 1) * WINDOW, :]
    k_cur = _head_pad_variants(_rope(qkv[:, COL_KV:COL_KV + KV_WIDTH], cos_t, sin_t))
    v_cur = _head_pad_variants(qkv[:, COL_KV + KV_WIDTH:])
    for n in range(2 * N_KV_HEADS):
        kbuf[n, WINDOW:, :] = k_cur[n].astype(BF16)
        vbuf[n, WINDOW:, :LANES] = v_cur[n].astype(BF16)

    ga_s[...] = jax.nn.silu(proj(COL_GATTN, ATTN_WIDTH))
    gc_s[...] = proj(COL_GCONV, CONV_WIDTH)

    row = lax.broadcasted_iota(jnp.int32, (stack, WINDOW), 0) % WINDOW
    tri = lax.broadcasted_iota(jnp.int32, (stack, WINDOW), 1) <= row
    nt = (((1,), (1,)), ((), ()))
    rows_of = [slice(n * WINDOW, (n + 1) * WINDOW) for n in range(GQA_GROUP)]
    even_lanes = lax.broadcasted_iota(jnp.int32, (WINDOW, LANES), 1) < HEAD_DIM
    for b in range(nblk):
        r0 = b * WINDOW
        for g in range(N_KV_HEADS):
            q0 = (b * N_KV_HEADS + g) * pairs * WINDOW
            q4 = qbuf[q0:q0 + pairs * WINDOW, :]
            s2 = jnp.concatenate(
                [lax.dot_general(q4, kbuf[2 * g + e, r0:r0 + 2 * WINDOW, :], nt, preferred_element_type=F32)
                 for e in range(2)], axis=0)
            s_prev = s2[:, :WINDOW]
            if b == 0:
                s_prev = jnp.where(i > 0, s_prev, NEG_INF)
            s = jnp.where(tri, s2[:, WINDOW:], s_prev)
            heads = [g * GQA_GROUP + 2 * pr + e for e in range(2) for pr in range(pairs)]
            s_max = jnp.max(s, axis=-1, keepdims=True)
            m = [jnp.maximum(s_max[rs], sinks_ref[h]) for rs, h in zip(rows_of, heads)]
            ex = jnp.concatenate([jnp.exp(s[rs] - mh) for rs, mh in zip(rows_of, m)], axis=0)
            p2 = jnp.concatenate([jnp.where(tri, 0.0, ex), jnp.where(tri, ex, 0.0)], axis=1).astype(BF16)
            half = pairs * WINDOW
            o = (jnp.dot(p2[:half], vbuf[2 * g, r0:r0 + 2 * WINDOW, :], preferred_element_type=F32)
                 + jnp.dot(p2[half:], vbuf[2 * g + 1, r0:r0 + 2 * WINDOW, :], preferred_element_type=F32))
            sink_term = [jnp.exp(sinks_ref[h] - mh) for h, mh in zip(heads, m)]
            for pr in range(pairs):
                c = g * pairs + pr
                rs = rows_of[pr]
                denom = o[rs, LANES:] + jnp.where(even_lanes, sink_term[pr], sink_term[pairs + pr])
                ga = ga_s[r0:r0 + WINDOW, c * LANES:(c + 1) * LANES]
                u_ref[r0:r0 + WINDOW, c * LANES:(c + 1) * LANES] = (
                    o[rs, :LANES] * (1.0 / denom) * ga).astype(BF16)

    lead = CONV_HALO - (CONV_KERNEL - 1)
    for c in range(lane_tiles):
        cs = slice(c * LANES, (c + 1) * LANES)
        w_taps = [jnp.broadcast_to(wdw_ref[j:j + 1, cs], (SUBLANES, LANES)) for j in range(CONV_KERNEL)]
        bias = jnp.broadcast_to(bdw_ref[:, cs], (SUBLANES, LANES))
        for r in range(seg):
            acc = bias
            for j in range(CONV_KERNEL):
                acc = acc + hbuf[c, pl.ds(r + lead + j, SUBLANES, stride=seg), :] * w_taps[j]
            cb[c, pl.ds(r, SUBLANES, stride=seg), :] = acc

    for r0 in range(0, tile, ln_rows):
        x = jnp.concatenate([cb[c, r0:r0 + ln_rows, :] for c in range(lane_tiles)], axis=1)
        mu = jnp.mean(x, axis=-1, keepdims=True)
        xc = x - mu
        var = jnp.mean(xc * xc, axis=-1, keepdims=True)
        y = xc * lax.rsqrt(var + LN_EPS) * lng_ref[...] + lnb_ref[...]
        cbuf[r0:r0 + ln_rows, :] = jax.nn.silu(y).astype(BF16)

    pw = jnp.dot(cbuf[...], wpw_ref[...], preferred_element_type=F32) + bpw_ref[...]
    u_ref[:, ATTN_WIDTH:] = (pw * jax.nn.silu(gc_s[...])).astype(BF16)


def _proj_mixers(x2d, g, w_in, b_in, cos_t, sin_t, sinks, w_dw, b_dw, ln_g, ln_b, w_pw, b_pw, *, tile):
    s, d = x2d.shape
    seg = tile // SUBLANES + 1
    assert seg % 2 == 1 and SUBLANES * seg + CONV_KERNEL - 1 <= tile + 2 * CONV_HALO

    def resident(a):
        return pl.BlockSpec(a.shape, lambda i: (0,) * a.ndim, pipeline_mode=pl.Buffered(1))

    rows = pl.BlockSpec((tile, LANES), lambda i: (i, 0))
    return pl.pallas_call(
        functools.partial(_proj_mixers_kernel, tile=tile, seg=seg, ln_rows=64, norm_rows=64),
        grid=(s // tile,),
        in_specs=[
            pl.BlockSpec((tile, d), lambda i: (i, 0)),
            resident(g), resident(w_in), resident(b_in),
            rows, rows,
            pl.BlockSpec(memory_space=pltpu.SMEM),
            resident(w_dw), resident(b_dw), resident(ln_g), resident(ln_b), resident(w_pw), resident(b_pw),
        ],
        out_specs=pl.BlockSpec((tile, D_MODEL), lambda i: (i, 0)),
        out_shape=jax.ShapeDtypeStruct((s, D_MODEL), BF16),
        scratch_shapes=[
            pltpu.VMEM((tile, d), BF16),
            pltpu.VMEM((tile * ATTN_WIDTH // LANES, LANES), BF16),
            pltpu.VMEM((2 * N_KV_HEADS, WINDOW + tile, LANES), BF16),
            pltpu.VMEM((2 * N_KV_HEADS, WINDOW + tile, 2 * LANES), BF16),
            pltpu.VMEM((tile, ATTN_WIDTH), F32),
            pltpu.VMEM((tile, CONV_WIDTH), F32),
            pltpu.VMEM((CONV_WIDTH // LANES, tile + 2 * CONV_HALO, LANES), F32),
            pltpu.VMEM((CONV_WIDTH // LANES, SUBLANES * seg, LANES), F32),
            pltpu.VMEM((tile, CONV_WIDTH), BF16),
        ],
        compiler_params=pltpu.CompilerParams(
            dimension_semantics=("arbitrary",), vmem_limit_bytes=VMEM_LIMIT),
        name="proj_mixers",
    )(x2d, g, w_in, b_in, cos_t, sin_t, sinks, w_dw, b_dw, ln_g, ln_b, w_pw, b_pw)


def _out_proj_kernel(u_ref, x_ref, w_ref, b_ref, g_ref, o_ref, *, rows):
    for r0 in range(0, u_ref.shape[0], rows):
        rs = slice(r0, r0 + rows)
        y = jnp.dot(u_ref[rs, :], w_ref[...], preferred_element_type=F32) + b_ref[...]
        ms = jnp.mean(y * y, axis=-1, keepdims=True)
        o_ref[rs, :] = x_ref[rs, :] + y * lax.rsqrt(ms + RMS_EPS) * g_ref[...]


def _out_proj(u, x2d, w, b, g, *, tm):
    s, d = x2d.shape
    return pl.pallas_call(
        functools.partial(_out_proj_kernel, rows=256),
        grid=(s // tm,),
        in_specs=[
            pl.BlockSpec((tm, u.shape[1]), lambda i: (i, 0)),
            pl.BlockSpec((tm, d), lambda i: (i, 0)),
            pl.BlockSpec(w.shape, lambda i: (0, 0)),
            pl.BlockSpec((1, d), lambda i: (0, 0)),
            pl.BlockSpec((1, d), lambda i: (0, 0)),
        ],
        out_specs=pl.BlockSpec((tm, d), lambda i: (i, 0)),
        out_shape=jax.ShapeDtypeStruct((s, d), F32),
        compiler_params=pltpu.CompilerParams(
            dimension_semantics=("parallel",), vmem_limit_bytes=VMEM_LIMIT),
        name="out_proj",
    )(u, x2d, w, b, g)


def kernel(x, positions, pre_norm_g, w_in, b_in, sinks, w_dw, b_dw, conv_ln_g, conv_ln_b,
           w_pw, b_pw, w_out, b_out, post_norm_g):
    bsz, seq, d = x.shape
    depth = w_in.shape[0]
    assert d == D_MODEL and w_in.shape[2] == IN_COLS and seq % 1024 == 0

    outs = []
    for bi in range(bsz):
        xb = x[bi]
        cos_t, sin_t = _rope_tables(positions[bi].astype(F32), chunk=4096)
        for l in range(depth):
            u = _proj_mixers(xb, pre_norm_g[l].reshape(1, d), w_in[l].astype(BF16), b_in[l].reshape(1, IN_COLS),
                             cos_t, sin_t, sinks[l], w_dw[l], b_dw[l].reshape(1, -1),
                             conv_ln_g[l].reshape(1, -1), conv_ln_b[l].reshape(1, -1),
                             w_pw[l].astype(BF16), b_pw[l].reshape(1, -1), tile=512)
            xb = _out_proj(u, xb, w_out[l].astype(BF16), b_out[l].reshape(1, d),
                           post_norm_g[l].reshape(1, d), tm=512)
        outs.append(xb)
    return outs[0].reshape(1, seq, d) if bsz == 1 else jnp.stack(outs, axis=0)
```

```python
import functools

import jax
import jax.numpy as jnp
from jax import lax
from jax.experimental import pallas as pl
from jax.experimental.pallas import tpu as pltpu

D_MODEL = 2048
CONV_WIDTH = 1024
ATTN_WIDTH = 1024
HEAD_DIM = 64
N_HEADS = 16
N_KV_HEADS = 2
GQA_GROUP = N_HEADS // N_KV_HEADS
KV_WIDTH = N_KV_HEADS * HEAD_DIM
CONV_KERNEL = 31
WINDOW = 128
ROPE_THETA = 500000.0
ROTARY_DIM = HEAD_DIM // 4
ROTARY_HALF = ROTARY_DIM // 2
RMS_EPS = 1e-6
LN_EPS = 1e-5
NEG_INF = -1e30
IN_COLS = 2 * ATTN_WIDTH + 2 * KV_WIDTH + 3 * CONV_WIDTH

COL_Q = 0
COL_KV = COL_Q + ATTN_WIDTH
COL_GATTN = COL_KV + 2 * KV_WIDTH
COL_VAL = COL_GATTN + ATTN_WIDTH
COL_GATE = COL_VAL + CONV_WIDTH
COL_GCONV = COL_GATE + CONV_WIDTH

LANES = 128
SUBLANES = 8
MXU_COLS = 256
CONV_HALO = 32
VMEM_LIMIT = 56 * 1024 * 1024

F32 = jnp.float32
BF16 = jnp.bfloat16


def _rope_table_kernel(pos_ref, invf_ref, sel_ref, cos_ref, sin_ref, *, chunk):
    ang = invf_ref[...] * pos_ref[...]
    cos_a, sin_a = jnp.cos(ang), jnp.sin(ang)
    sel = sel_ref[...]
    rotary = jnp.sum(sel, axis=0, keepdims=True)
    tn = (((0,), (0,)), ((), ()))
    for k in range(chunk // LANES):
        cols = slice(k * LANES, (k + 1) * LANES)
        rows = slice(k * LANES, (k + 1) * LANES)
        cos_ref[rows, :] = lax.dot_general(cos_a[:, cols], sel, tn, precision=lax.Precision.HIGHEST,
                                           preferred_element_type=F32) + (1.0 - rotary)
        sin_ref[rows, :] = lax.dot_general(sin_a[:, cols], sel, tn, precision=lax.Precision.HIGHEST,
                                           preferred_element_type=F32)


def _rope_tables(pos, *, chunk):
    seq = pos.shape[0]
    freq = jnp.arange(ROTARY_HALF, dtype=F32)
    invf = (ROPE_THETA ** (-(2 * freq) / ROTARY_DIM)).reshape(ROTARY_HALF, 1)
    d = jnp.arange(LANES) % HEAD_DIM
    sel = ((d[None, :] % ROTARY_HALF == jnp.arange(ROTARY_HALF)[:, None]) & (d[None, :] < ROTARY_DIM)).astype(F32)
    out = jax.ShapeDtypeStruct((seq, LANES), F32)
    return pl.pallas_call(
        functools.partial(_rope_table_kernel, chunk=chunk),
        grid=(seq // chunk,),
        in_specs=[
            pl.BlockSpec((1, chunk), lambda i: (0, i)),
            pl.BlockSpec((ROTARY_HALF, 1), lambda i: (0, 0)),
            pl.BlockSpec((ROTARY_HALF, LANES), lambda i: (0, 0)),
        ],
        out_specs=[pl.BlockSpec((chunk, LANES), lambda i: (i, 0))] * 2,
        out_shape=(out, out),
        compiler_params=pltpu.CompilerParams(dimension_semantics=("parallel",)),
        name="rope_tables",
    )(pos.reshape(1, seq), invf, sel)


def _rope(xc, cos_t, sin_t):
    d = lax.broadcasted_iota(jnp.int32, xc.shape, 1) % HEAD_DIM
    lo = d < ROTARY_HALF
    partner = jnp.where(lo, pltpu.roll(xc, LANES - ROTARY_HALF, 1),
                        pltpu.roll(xc, ROTARY_HALF, 1))
    return xc * cos_t + partner * jnp.where(lo, -sin_t, sin_t)


def _head_pad_variants(a):
    lo = lax.broadcasted_iota(jnp.int32, a.shape, 1) < HEAD_DIM
    r = pltpu.roll(a, HEAD_DIM, 1)
    return (jnp.where(lo, a, 0.0), jnp.where(lo, 0.0, r),
            jnp.where(lo, r, 0.0), jnp.where(lo, 0.0, a))


def _proj_mixers_kernel(x_ref, g_ref, win_ref, bin_ref, cos_ref, sin_ref, sinks_ref, wdw_ref, bdw_ref,
                        lng_ref, lnb_ref, wpw_ref, bpw_ref, u_ref,
                        hn, qbuf, kbuf, vbuf, ga_s, gc_s, hbuf, cb, cbuf, *, tile, seg, ln_rows, norm_rows):
    i = pl.program_id(0)
    nblk = tile // WINDOW
    pairs = GQA_GROUP // 2
    stack = GQA_GROUP * WINDOW
    lane_tiles = CONV_WIDTH // LANES

    @pl.when(i == 0)
    def _():
        kbuf[...] = jnp.zeros(kbuf.shape, BF16)
        hbuf[...] = jnp.zeros(hbuf.shape, F32)
        lane = lax.broadcasted_iota(jnp.int32, (WINDOW + tile, LANES), 1)
        for n in range(2 * N_KV_HEADS):
            vbuf[n, :, :LANES] = jnp.zeros((WINDOW + tile, LANES), BF16)
            vbuf[n, :, LANES:] = ((lane < HEAD_DIM) == (n % 2 == 0)).astype(BF16)

    for n in range(2 * N_KV_HEADS):
        kbuf[n, :WINDOW, :] = kbuf[n, tile:, :]
        vbuf[n, :WINDOW, :LANES] = vbuf[n, tile:, :LANES]
    for c in range(lane_tiles):
        hbuf[c, :CONV_HALO, :] = hbuf[c, tile:tile + CONV_HALO, :]

    for r0 in range(0, tile, norm_rows):
        x = x_ref[r0:r0 + norm_rows, :]
        ms = jnp.mean(x * x, axis=-1, keepdims=True)
        hn[r0:r0 + norm_rows, :] = (x * lax.rsqrt(ms + RMS_EPS) * g_ref[...]).astype(BF16)

    def proj(col0, width):
        return (jnp.dot(hn[...], win_ref[:, col0:col0 + width], preferred_element_type=F32)
                + bin_ref[:, col0:col0 + width])

    for k in range(CONV_WIDTH // MXU_COLS):
        val = proj(COL_VAL + k * MXU_COLS, MXU_COLS)
        gate = proj(COL_GATE + k * MXU_COLS, MXU_COLS)
        for t in range(MXU_COLS // LANES):
            hbuf[k * (MXU_COLS // LANES) + t, CONV_HALO:CONV_HALO + tile, :] = (
                val[:, t * LANES:(t + 1) * LANES] * jax.nn.sigmoid(gate[:, t * LANES:(t + 1) * LANES]))

    cos_t, sin_t = cos_ref[...], sin_ref[...]
    scale = HEAD_DIM ** -0.5
    per_chunk = MXU_COLS // LANES

    kv = proj(COL_KV, 2 * KV_WIDTH)
    k_cur = _head_pad_variants(_rope(kv[:, :KV_WIDTH], cos_t, sin_t))
    v_cur = _head_pad_variants(kv[:, KV_WIDTH:])
    for n in range(2 * N_KV_HEADS):
        kbuf[n, WINDOW:, :] = k_cur[n].astype(BF16)
        vbuf[n, WINDOW:, :LANES] = v_cur[n].astype(BF16)

    def q_chunk(k):
        q = proj(COL_Q + k * MXU_COLS, MXU_COLS)
        for t in range(per_chunk):
            c = k * per_chunk + t
            qc = (_rope(q[:, t * LANES:(t + 1) * LANES], cos_t, sin_t) * scale).astype(BF16)
            for b in range(nblk):
                dst = (b * N_KV_HEADS * pairs + c) * WINDOW
                qbuf[dst:dst + WINDOW, :] = qc[b * WINDOW:(b + 1) * WINDOW, :]

    def ga_chunk(k):
        cols = slice(k * MXU_COLS, (k + 1) * MXU_COLS)
        ga_s[:, cols] = jax.nn.silu(proj(COL_GATTN + k * MXU_COLS, MXU_COLS))

    def gc_chunk(k):
        cols = slice(k * MXU_COLS, (k + 1) * MXU_COLS)
        gc_s[:, cols] = jax.nn.silu(proj(COL_GCONV + k * MXU_COLS, MXU_COLS))

    row = lax.broadcasted_iota(jnp.int32, (stack, WINDOW), 0) % WINDOW
    tri = lax.broadcasted_iota(jnp.int32, (stack, WINDOW), 1) <= row
    nt = (((1,), (1,)), ((), ()))
    rows_of = [slice(n * WINDOW, (n + 1) * WINDOW) for n in range(GQA_GROUP)]
    even_lanes = lax.broadcasted_iota(jnp.int32, (WINDOW, LANES), 1) < HEAD_DIM
    half = pairs * WINDOW

    def scores(b, g):
        r0 = b * WINDOW
        q0 = (b * N_KV_HEADS + g) * pairs * WINDOW
        q4 = qbuf[q0:q0 + pairs * WINDOW, :]
        s2 = jnp.concatenate(
            [lax.dot_general(q4, kbuf[2 * g + e, r0:r0 + 2 * WINDOW, :], nt, preferred_element_type=F32)
             for e in range(2)], axis=0)
        s_prev = s2[:, :WINDOW]
        if b == 0:
            s_prev = jnp.where(i > 0, s_prev, NEG_INF)
        s = jnp.where(tri, s2[:, WINDOW:], s_prev)
        heads = [g * GQA_GROUP + 2 * pr + e for e in range(2) for pr in range(pairs)]
        s_max = jnp.max(s, axis=-1, keepdims=True)
        m = [jnp.maximum(s_max[rs], sinks_ref[h]) for rs, h in zip(rows_of, heads)]
        ex = jnp.concatenate([jnp.exp(s[rs] - mh) for rs, mh in zip(rows_of, m)], axis=0)
        p2 = jnp.concatenate([jnp.where(tri, 0.0, ex), jnp.where(tri, ex, 0.0)], axis=1).astype(BF16)
        return p2, [jnp.exp(sinks_ref[h] - mh) for h, mh in zip(heads, m)]

    def outputs(b, g, p2, sink_term):
        r0 = b * WINDOW
        o = (jnp.dot(p2[:half], vbuf[2 * g, r0:r0 + 2 * WINDOW, :], preferred_element_type=F32)
             + jnp.dot(p2[half:], vbuf[2 * g + 1, r0:r0 + 2 * WINDOW, :], preferred_element_type=F32))
        for pr in range(pairs):
            c = g * pairs + pr
            rs = rows_of[pr]
            denom = o[rs, LANES:] + jnp.where(even_lanes, sink_term[pr], sink_term[pairs + pr])
            ga = ga_s[r0:r0 + WINDOW, c * LANES:(c + 1) * LANES]
            u_ref[r0:r0 + WINDOW, c * LANES:(c + 1) * LANES] = (
                o[rs, :LANES] * (1.0 / denom) * ga).astype(BF16)

    chunks_per_group = ATTN_WIDTH // MXU_COLS // N_KV_HEADS
    for k in range(chunks_per_group):
        q_chunk(k)
    for k in range(chunks_per_group):
        ga_chunk(k)
    fillers = ([functools.partial(q_chunk, k) for k in range(chunks_per_group, 2 * chunks_per_group)]
               + [functools.partial(ga_chunk, k) for k in range(chunks_per_group, 2 * chunks_per_group)]
               + [functools.partial(gc_chunk, k) for k in range(CONV_WIDTH // MXU_COLS)])
    assert nblk >= 2 * chunks_per_group
    for g in range(N_KV_HEADS):
        for b in range(nblk):
            state = scores(b, g)
            if fillers:
                fillers.pop(0)()
            outputs(b, g, *state)
    for f in fillers:
        f()

    lead = CONV_HALO - (CONV_KERNEL - 1)
    for c in range(lane_tiles):
        cs = slice(c * LANES, (c + 1) * LANES)
        w_taps = [jnp.broadcast_to(wdw_ref[j:j + 1, cs], (SUBLANES, LANES)) for j in range(CONV_KERNEL)]
        bias = jnp.broadcast_to(bdw_ref[:, cs], (SUBLANES, LANES))
        for r in range(seg):
            acc = bias
            for j in range(CONV_KERNEL):
                acc = acc + hbuf[c, pl.ds(r + lead + j, SUBLANES, stride=seg), :] * w_taps[j]
            cb[c, pl.ds(r, SUBLANES, stride=seg), :] = acc

    for r0 in range(0, tile, ln_rows):
        x = jnp.concatenate([cb[c, r0:r0 + ln_rows, :] for c in range(lane_tiles)], axis=1)
        mu = jnp.mean(x, axis=-1, keepdims=True)
        xc = x - mu
        var = jnp.mean(xc * xc, axis=-1, keepdims=True)
        y = xc * lax.rsqrt(var + LN_EPS) * lng_ref[...] + lnb_ref[...]
        cbuf[r0:r0 + ln_rows, :] = jax.nn.silu(y).astype(BF16)

    pw = jnp.dot(cbuf[...], wpw_ref[...], preferred_element_type=F32) + bpw_ref[...]
    u_ref[:, ATTN_WIDTH:] = (pw * gc_s[...]).astype(BF16)


def _proj_mixers(x2d, g, w_in, b_in, cos_t, sin_t, sinks, w_dw, b_dw, ln_g, ln_b, w_pw, b_pw, *, tile):
    s, d = x2d.shape
    seg = tile // SUBLANES + 1
    assert seg % 2 == 1 and SUBLANES * seg + CONV_KERNEL - 1 <= tile + 2 * CONV_HALO

    def resident(a):
        return pl.BlockSpec(a.shape, lambda i: (0,) * a.ndim, pipeline_mode=pl.Buffered(1))

    rows = pl.BlockSpec((tile, LANES), lambda i: (i, 0))
    return pl.pallas_call(
        functools.partial(_proj_mixers_kernel, tile=tile, seg=seg, ln_rows=64, norm_rows=64),
        grid=(s // tile,),
        in_specs=[
            pl.BlockSpec((tile, d), lambda i: (i, 0)),
            resident(g), resident(w_in), resident(b_in),
            rows, rows,
            pl.BlockSpec(memory_space=pltpu.SMEM),
            resident(w_dw), resident(b_dw), resident(ln_g), resident(ln_b), resident(w_pw), resident(b_pw),
        ],
        out_specs=pl.BlockSpec((tile, D_MODEL), lambda i: (i, 0)),
        out_shape=jax.ShapeDtypeStruct((s, D_MODEL), BF16),
        scratch_shapes=[
            pltpu.VMEM((tile, d), BF16),
            pltpu.VMEM((tile * ATTN_WIDTH // LANES, LANES), BF16),
            pltpu.VMEM((2 * N_KV_HEADS, WINDOW + tile, LANES), BF16),
            pltpu.VMEM((2 * N_KV_HEADS, WINDOW + tile, 2 * LANES), BF16),
            pltpu.VMEM((tile, ATTN_WIDTH), F32),
            pltpu.VMEM((tile, CONV_WIDTH), F32),
            pltpu.VMEM((CONV_WIDTH // LANES, tile + 2 * CONV_HALO, LANES), F32),
            pltpu.VMEM((CONV_WIDTH // LANES, SUBLANES * seg, LANES), F32),
            pltpu.VMEM((tile, CONV_WIDTH), BF16),
        ],
        compiler_params=pltpu.CompilerParams(
            dimension_semantics=("arbitrary",), vmem_limit_bytes=VMEM_LIMIT),
        name="proj_mixers",
    )(x2d, g, w_in, b_in, cos_t, sin_t, sinks, w_dw, b_dw, ln_g, ln_b, w_pw, b_pw)


def _out_proj_kernel(u_ref, x_ref, w_ref, b_ref, g_ref, o_ref, *, rows):
    for r0 in range(0, u_ref.shape[0], rows):
        rs = slice(r0, r0 + rows)
        y = jnp.dot(u_ref[rs, :], w_ref[...], preferred_element_type=F32) + b_ref[...]
        ms = jnp.mean(y * y, axis=-1, keepdims=True)
        o_ref[rs, :] = x_ref[rs, :] + y * lax.rsqrt(ms + RMS_EPS) * g_ref[...]


def _out_proj(u, x2d, w, b, g, *, tm, rows):
    s, d = x2d.shape
    return pl.pallas_call(
        functools.partial(_out_proj_kernel, rows=rows),
        grid=(s // tm,),
        in_specs=[
            pl.BlockSpec((tm, u.shape[1]), lambda i: (i, 0)),
            pl.BlockSpec((tm, d), lambda i: (i, 0)),
            pl.BlockSpec(w.shape, lambda i: (0, 0), pipeline_mode=pl.Buffered(1)),
            pl.BlockSpec((1, d), lambda i: (0, 0)),
            pl.BlockSpec((1, d), lambda i: (0, 0)),
        ],
        out_specs=pl.BlockSpec((tm, d), lambda i: (i, 0)),
        out_shape=jax.ShapeDtypeStruct((s, d), F32),
        compiler_params=pltpu.CompilerParams(
            dimension_semantics=("parallel",), vmem_limit_bytes=VMEM_LIMIT),
        name="out_proj",
    )(u, x2d, w, b, g)


def kernel(x, positions, pre_norm_g, w_in, b_in, sinks, w_dw, b_dw, conv_ln_g, conv_ln_b,
           w_pw, b_pw, w_out, b_out, post_norm_g):
    bsz, seq, d = x.shape
    depth = w_in.shape[0]
    assert d == D_MODEL and w_in.shape[2] == IN_COLS and seq % 1024 == 0

    outs = []
    for bi in range(bsz):
        xb = x[bi]
        cos_t, sin_t = _rope_tables(positions[bi].astype(F32), chunk=4096)
        for l in range(depth):
            u = _proj_mixers(xb, pre_norm_g[l].reshape(1, d), w_in[l].astype(BF16), b_in[l].reshape(1, IN_COLS),
                             cos_t, sin_t, sinks[l], w_dw[l], b_dw[l].reshape(1, -1),
                             conv_ln_g[l].reshape(1, -1), conv_ln_b[l].reshape(1, -1),
                             w_pw[l].astype(BF16), b_pw[l].reshape(1, -1), tile=512)
            xb = _out_proj(u, xb, w_out[l].astype(BF16), b_out[l].reshape(1, d),
                           post_norm_g[l].reshape(1, d), tm=512, rows=512)
        outs.append(xb)
    return outs[0].reshape(1, seq, d) if bsz == 1 else jnp.stack(outs, axis=0)
```

```python
import functools

import jax
import jax.numpy as jnp
from jax import lax
from jax.experimental import pallas as pl
from jax.experimental.pallas import tpu as pltpu

D_MODEL = 2048
CONV_WIDTH = 1024
ATTN_WIDTH = 1024
HEAD_DIM = 64
N_HEADS = 16
N_KV_HEADS = 2
GQA_GROUP = N_HEADS // N_KV_HEADS
KV_WIDTH = N_KV_HEADS * HEAD_DIM
CONV_KERNEL = 31
WINDOW = 128
ROPE_THETA = 500000.0
ROTARY_DIM = HEAD_DIM // 4
ROTARY_HALF = ROTARY_DIM // 2
RMS_EPS = 1e-6
LN_EPS = 1e-5
NEG_INF = -1e30
IN_COLS = 2 * ATTN_WIDTH + 2 * KV_WIDTH + 3 * CONV_WIDTH

COL_Q = 0
COL_KV = COL_Q + ATTN_WIDTH
COL_GATTN = COL_KV + 2 * KV_WIDTH
COL_VAL = COL_GATTN + ATTN_WIDTH
COL_GATE = COL_VAL + CONV_WIDTH
COL_GCONV = COL_GATE + CONV_WIDTH

LANES = 128
SUBLANES = 8
MXU_COLS = 256
CONV_HALO = 32
V7X_VMEM_BYTES = 64 * 1024 * 1024
VMEM_LIMIT = V7X_VMEM_BYTES // 8 * 7

F32 = jnp.float32
BF16 = jnp.bfloat16


def _rope_table_kernel(pos_ref, invf_ref, sel_ref, cos_ref, sin_ref, *, chunk):
    ang = invf_ref[...] * pos_ref[...]
    cos_a, sin_a = jnp.cos(ang), jnp.sin(ang)
    sel = sel_ref[...]
    rotary = jnp.sum(sel, axis=0, keepdims=True)
    tn = (((0,), (0,)), ((), ()))
    for k in range(chunk // LANES):
        cols = slice(k * LANES, (k + 1) * LANES)
        rows = slice(k * LANES, (k + 1) * LANES)
        cos_ref[rows, :] = lax.dot_general(cos_a[:, cols], sel, tn, precision=lax.Precision.HIGHEST,
                                           preferred_element_type=F32) + (1.0 - rotary)
        sin_ref[rows, :] = lax.dot_general(sin_a[:, cols], sel, tn, precision=lax.Precision.HIGHEST,
                                           preferred_element_type=F32)


def _rope_tables(pos, *, chunk):
    seq = pos.shape[0]
    freq = jnp.arange(ROTARY_HALF, dtype=F32)
    invf = (ROPE_THETA ** (-(2 * freq) / ROTARY_DIM)).reshape(ROTARY_HALF, 1)
    d = jnp.arange(LANES) % HEAD_DIM
    sel = ((d[None, :] % ROTARY_HALF == jnp.arange(ROTARY_HALF)[:, None]) & (d[None, :] < ROTARY_DIM)).astype(F32)
    out = jax.ShapeDtypeStruct((seq, LANES), F32)
    return pl.pallas_call(
        functools.partial(_rope_table_kernel, chunk=chunk),
        grid=(seq // chunk,),
        in_specs=[
            pl.BlockSpec((1, chunk), lambda i: (0, i)),
            pl.BlockSpec((ROTARY_HALF, 1), lambda i: (0, 0)),
            pl.BlockSpec((ROTARY_HALF, LANES), lambda i: (0, 0)),
        ],
        out_specs=[pl.BlockSpec((chunk, LANES), lambda i: (i, 0))] * 2,
        out_shape=(out, out),
        compiler_params=pltpu.CompilerParams(dimension_semantics=("parallel",)),
        name="rope_tables",
    )(pos.reshape(1, seq), invf, sel)


def _rope(xc, cos_t, sin_t):
    d = lax.broadcasted_iota(jnp.int32, xc.shape, 1) % HEAD_DIM
    lo = d < ROTARY_HALF
    partner = jnp.where(lo, pltpu.roll(xc, LANES - ROTARY_HALF, 1),
                        pltpu.roll(xc, ROTARY_HALF, 1))
    return xc * cos_t + partner * jnp.where(lo, -sin_t, sin_t)


def _head_pad_variants(a):
    lo = lax.broadcasted_iota(jnp.int32, a.shape, 1) < HEAD_DIM
    r = pltpu.roll(a, HEAD_DIM, 1)
    return (jnp.where(lo, a, 0.0), jnp.where(lo, 0.0, r),
            jnp.where(lo, r, 0.0), jnp.where(lo, 0.0, a))


def _proj_mixers_kernel(x_ref, g_ref, win_ref, bin_ref, cos_ref, sin_ref, sinks_ref, wdw_ref, bdw_ref,
                        lng_ref, lnb_ref, wpw_ref, bpw_ref, u_ref,
                        hn, qbuf, kbuf, vbuf, ga_s, gc_s, hbuf, cb, cbuf, *, tile, seg, ln_rows, norm_rows, pw_rows):
    i = pl.program_id(0)
    nblk = tile // WINDOW
    pairs = GQA_GROUP // 2
    stack = GQA_GROUP * WINDOW
    lane_tiles = CONV_WIDTH // LANES

    @pl.when(i == 0)
    def _():
        kbuf[...] = jnp.zeros(kbuf.shape, BF16)
        hbuf[...] = jnp.zeros(hbuf.shape, F32)
        lane = lax.broadcasted_iota(jnp.int32, (WINDOW + tile, LANES), 1)
        for n in range(2 * N_KV_HEADS):
            vbuf[n, :, :LANES] = jnp.zeros((WINDOW + tile, LANES), BF16)
            vbuf[n, :, LANES:] = ((lane < HEAD_DIM) == (n % 2 == 0)).astype(BF16)

    for n in range(2 * N_KV_HEADS):
        kbuf[n, :WINDOW, :] = kbuf[n, tile:, :]
        vbuf[n, :WINDOW, :LANES] = vbuf[n, tile:, :LANES]
    for c in range(lane_tiles):
        hbuf[c, :CONV_HALO, :] = hbuf[c, tile:tile + CONV_HALO, :]

    for r0 in range(0, tile, norm_rows):
        x = x_ref[r0:r0 + norm_rows, :]
        ms = jnp.mean(x * x, axis=-1, keepdims=True)
        hn[r0:r0 + norm_rows, :] = (x * lax.rsqrt(ms + RMS_EPS) * g_ref[...]).astype(BF16)

    def proj(col0, width):
        return (jnp.dot(hn[...], win_ref[:, col0:col0 + width], preferred_element_type=F32)
                + bin_ref[:, col0:col0 + width])

    for k in range(CONV_WIDTH // MXU_COLS):
        val = proj(COL_VAL + k * MXU_COLS, MXU_COLS)
        gate = proj(COL_GATE + k * MXU_COLS, MXU_COLS)
        for t in range(MXU_COLS // LANES):
            hbuf[k * (MXU_COLS // LANES) + t, CONV_HALO:CONV_HALO + tile, :] = (
                val[:, t * LANES:(t + 1) * LANES] * jax.nn.sigmoid(gate[:, t * LANES:(t + 1) * LANES]))

    cos_t, sin_t = cos_ref[...], sin_ref[...]
    scale = HEAD_DIM ** -0.5
    per_chunk = MXU_COLS // LANES

    kv = proj(COL_KV, 2 * KV_WIDTH)
    k_cur = _head_pad_variants(_rope(kv[:, :KV_WIDTH], cos_t, sin_t))
    v_cur = _head_pad_variants(kv[:, KV_WIDTH:])
    for n in range(2 * N_KV_HEADS):
        kbuf[n, WINDOW:, :] = k_cur[n].astype(BF16)
        vbuf[n, WINDOW:, :LANES] = v_cur[n].astype(BF16)

    def q_chunk(k):
        q = proj(COL_Q + k * MXU_COLS, MXU_COLS)
        for t in range(per_chunk):
            c = k * per_chunk + t
            qc = (_rope(q[:, t * LANES:(t + 1) * LANES], cos_t, sin_t) * scale).astype(BF16)
            for b in range(nblk):
                dst = (b * N_KV_HEADS * pairs + c) * WINDOW
                qbuf[dst:dst + WINDOW, :] = qc[b * WINDOW:(b + 1) * WINDOW, :]

    def ga_chunk(k):
        cols = slice(k * MXU_COLS, (k + 1) * MXU_COLS)
        ga_s[:, cols] = jax.nn.silu(proj(COL_GATTN + k * MXU_COLS, MXU_COLS))

    def gc_chunk(k):
        cols = slice(k * MXU_COLS, (k + 1) * MXU_COLS)
        gc_s[:, cols] = jax.nn.silu(proj(COL_GCONV + k * MXU_COLS, MXU_COLS))

    row = lax.broadcasted_iota(jnp.int32, (stack, WINDOW), 0) % WINDOW
    tri = lax.broadcasted_iota(jnp.int32, (stack, WINDOW), 1) <= row
    nt = (((1,), (1,)), ((), ()))
    rows_of = [slice(n * WINDOW, (n + 1) * WINDOW) for n in range(GQA_GROUP)]
    even_lanes = lax.broadcasted_iota(jnp.int32, (WINDOW, LANES), 1) < HEAD_DIM
    half = pairs * WINDOW

    def scores(b, g):
        r0 = b * WINDOW
        q0 = (b * N_KV_HEADS + g) * pairs * WINDOW
        q4 = qbuf[q0:q0 + pairs * WINDOW, :]
        s2 = jnp.concatenate(
            [lax.dot_general(q4, kbuf[2 * g + e, r0:r0 + 2 * WINDOW, :], nt, preferred_element_type=F32)
             for e in range(2)], axis=0)
        s_prev = s2[:, :WINDOW]
        if b == 0:
            s_prev = jnp.where(i > 0, s_prev, NEG_INF)
        s = jnp.where(tri, s2[:, WINDOW:], s_prev)
        heads = [g * GQA_GROUP + 2 * pr + e for e in range(2) for pr in range(pairs)]
        s_max = jnp.max(s, axis=-1, keepdims=True)
        m = [jnp.maximum(s_max[rs], sinks_ref[h]) for rs, h in zip(rows_of, heads)]
        ex = jnp.concatenate([jnp.exp(s[rs] - mh) for rs, mh in zip(rows_of, m)], axis=0)
        p2 = jnp.concatenate([jnp.where(tri, 0.0, ex), jnp.where(tri, ex, 0.0)], axis=1).astype(BF16)
        return p2, [jnp.exp(sinks_ref[h] - mh) for h, mh in zip(heads, m)]

    def outputs(b, g, p2, sink_term):
        r0 = b * WINDOW
        o = (jnp.dot(p2[:half], vbuf[2 * g, r0:r0 + 2 * WINDOW, :], preferred_element_type=F32)
             + jnp.dot(p2[half:], vbuf[2 * g + 1, r0:r0 + 2 * WINDOW, :], preferred_element_type=F32))
        for pr in range(pairs):
            c = g * pairs + pr
            rs = rows_of[pr]
            denom = o[rs, LANES:] + jnp.where(even_lanes, sink_term[pr], sink_term[pairs + pr])
            ga = ga_s[r0:r0 + WINDOW, c * LANES:(c + 1) * LANES]
            u_ref[r0:r0 + WINDOW, c * LANES:(c + 1) * LANES] = (
                o[rs, :LANES] * (1.0 / denom) * ga).astype(BF16)

    chunks_per_group = ATTN_WIDTH // MXU_COLS // N_KV_HEADS
    for k in range(chunks_per_group):
        q_chunk(k)
    for k in range(chunks_per_group):
        ga_chunk(k)
    fillers = ([functools.partial(q_chunk, k) for k in range(chunks_per_group, 2 * chunks_per_group)]
               + [functools.partial(ga_chunk, k) for k in range(chunks_per_group, 2 * chunks_per_group)]
               + [functools.partial(gc_chunk, k) for k in range(CONV_WIDTH // MXU_COLS)])
    assert nblk >= 2 * chunks_per_group
    for g in range(N_KV_HEADS):
        for b in range(nblk):
            state = scores(b, g)
            if fillers:
                fillers.pop(0)()
            outputs(b, g, *state)
    for f in fillers:
        f()

    lead = CONV_HALO - (CONV_KERNEL - 1)
    for c in range(lane_tiles):
        cs = slice(c * LANES, (c + 1) * LANES)
        w_taps = [jnp.broadcast_to(wdw_ref[j:j + 1, cs], (SUBLANES, LANES)) for j in range(CONV_KERNEL)]
        bias = jnp.broadcast_to(bdw_ref[:, cs], (SUBLANES, LANES))
        for r in range(seg):
            acc = bias
            for j in range(CONV_KERNEL):
                acc = acc + hbuf[c, pl.ds(r + lead + j, SUBLANES, stride=seg), :] * w_taps[j]
            cb[c, pl.ds(r, SUBLANES, stride=seg), :] = acc

    for r0 in range(0, tile, ln_rows):
        x = jnp.concatenate([cb[c, r0:r0 + ln_rows, :] for c in range(lane_tiles)], axis=1)
        mu = jnp.mean(x, axis=-1, keepdims=True)
        xc = x - mu
        var = jnp.mean(xc * xc, axis=-1, keepdims=True)
        y = xc * lax.rsqrt(var + LN_EPS) * lng_ref[...] + lnb_ref[...]
        cbuf[r0:r0 + ln_rows, :] = jax.nn.silu(y).astype(BF16)

    for r0 in range(0, tile, pw_rows):
        rs = slice(r0, r0 + pw_rows)
        pw = jnp.dot(cbuf[rs, :], wpw_ref[...], preferred_element_type=F32) + bpw_ref[...]
        u_ref[rs, ATTN_WIDTH:] = (pw * gc_s[rs, :]).astype(BF16)


def _proj_mixers(x2d, g, w_in, b_in, cos_t, sin_t, sinks, w_dw, b_dw, ln_g, ln_b, w_pw, b_pw, *, tile):
    s, d = x2d.shape
    seg = tile // SUBLANES + 1
    assert seg % 2 == 1 and SUBLANES * seg + CONV_KERNEL - 1 <= tile + 2 * CONV_HALO

    def resident(a):
        return pl.BlockSpec(a.shape, lambda i: (0,) * a.ndim, pipeline_mode=pl.Buffered(1))

    rows = pl.BlockSpec((tile, LANES), lambda i: (i, 0))
    return pl.pallas_call(
        functools.partial(_proj_mixers_kernel, tile=tile, seg=seg, ln_rows=64, norm_rows=64, pw_rows=256),
        grid=(s // tile,),
        in_specs=[
            pl.BlockSpec((tile, d), lambda i: (i, 0)),
            resident(g), resident(w_in), resident(b_in),
            rows, rows,
            pl.BlockSpec(memory_space=pltpu.SMEM),
            resident(w_dw), resident(b_dw), resident(ln_g), resident(ln_b), resident(w_pw), resident(b_pw),
        ],
        out_specs=pl.BlockSpec((tile, D_MODEL), lambda i: (i, 0)),
        out_shape=jax.ShapeDtypeStruct((s, D_MODEL), BF16),
        scratch_shapes=[
            pltpu.VMEM((tile, d), BF16),
            pltpu.VMEM((tile * ATTN_WIDTH // LANES, LANES), BF16),
            pltpu.VMEM((2 * N_KV_HEADS, WINDOW + tile, LANES), BF16),
            pltpu.VMEM((2 * N_KV_HEADS, WINDOW + tile, 2 * LANES), BF16),
            pltpu.VMEM((tile, ATTN_WIDTH), F32),
            pltpu.VMEM((tile, CONV_WIDTH), F32),
            pltpu.VMEM((CONV_WIDTH // LANES, tile + 2 * CONV_HALO, LANES), F32),
            pltpu.VMEM((CONV_WIDTH // LANES, SUBLANES * seg, LANES), F32),
            pltpu.VMEM((tile, CONV_WIDTH), BF16),
        ],
        compiler_params=pltpu.CompilerParams(
            dimension_semantics=("arbitrary",), vmem_limit_bytes=VMEM_LIMIT),
        name="proj_mixers",
    )(x2d, g, w_in, b_in, cos_t, sin_t, sinks, w_dw, b_dw, ln_g, ln_b, w_pw, b_pw)


def _out_proj_kernel(u_ref, x_ref, w_ref, b_ref, g_ref, o_ref, w_bf16, *, rows, cast_rows):
    @pl.when(pl.program_id(0) == 0)
    def _():
        for k0 in range(0, w_ref.shape[0], cast_rows):
            w_bf16[k0:k0 + cast_rows, :] = w_ref[k0:k0 + cast_rows, :].astype(BF16)

    for r0 in range(0, u_ref.shape[0], rows):
        rs = slice(r0, r0 + rows)
        y = jnp.dot(u_ref[rs, :], w_bf16[...], preferred_element_type=F32) + b_ref[...]
        ms = jnp.mean(y * y, axis=-1, keepdims=True)
        o_ref[rs, :] = x_ref[rs, :] + y * lax.rsqrt(ms + RMS_EPS) * g_ref[...]


def _out_proj(u, x2d, w, b, g, *, tm, rows):
    s, d = x2d.shape
    return pl.pallas_call(
        functools.partial(_out_proj_kernel, rows=rows, cast_rows=128),
        grid=(s // tm,),
        in_specs=[
            pl.BlockSpec((tm, u.shape[1]), lambda i: (i, 0)),
            pl.BlockSpec((tm, d), lambda i: (i, 0)),
            pl.BlockSpec(w.shape, lambda i: (0, 0), pipeline_mode=pl.Buffered(1)),
            pl.BlockSpec((1, d), lambda i: (0, 0)),
            pl.BlockSpec((1, d), lambda i: (0, 0)),
        ],
        out_specs=pl.BlockSpec((tm, d), lambda i: (i, 0)),
        out_shape=jax.ShapeDtypeStruct((s, d), F32),
        scratch_shapes=[pltpu.VMEM(w.shape, BF16)],
        compiler_params=pltpu.CompilerParams(
            dimension_semantics=("arbitrary",), vmem_limit_bytes=VMEM_LIMIT),
        name="out_proj",
    )(u, x2d, w, b, g)


def kernel(x, positions, pre_norm_g, w_in, b_in, sinks, w_dw, b_dw, conv_ln_g, conv_ln_b,
           w_pw, b_pw, w_out, b_out, post_norm_g):
    bsz, seq, d = x.shape
    depth = w_in.shape[0]
    assert d == D_MODEL and w_in.shape[2] == IN_COLS and seq % 1024 == 0

    outs = []
    for bi in range(bsz):
        xb = x[bi]
        cos_t, sin_t = _rope_tables(positions[bi].astype(F32), chunk=4096)
        for l in range(depth):
            u = _proj_mixers(xb, pre_norm_g[l].reshape(1, d), w_in[l].astype(BF16), b_in[l].reshape(1, IN_COLS),
                             cos_t, sin_t, sinks[l], w_dw[l], b_dw[l].reshape(1, -1),
                             conv_ln_g[l].reshape(1, -1), conv_ln_b[l].reshape(1, -1),
                             w_pw[l].astype(BF16), b_pw[l].reshape(1, -1), tile=512)
            xb = _out_proj(u, xb, w_out[l], b_out[l].reshape(1, d),
                           post_norm_g[l].reshape(1, d), tm=512, rows=512)
        outs.append(xb)
    return outs[0].reshape(1, seq, d) if bsz == 1 else jnp.stack(outs, axis=0)
```

```python
import functools

import jax
import jax.numpy as jnp
from jax import lax
from jax.experimental import pallas as pl
from jax.experimental.pallas import tpu as pltpu

D_MODEL = 2048
CONV_WIDTH = 1024
ATTN_WIDTH = 1024
HEAD_DIM = 64
N_HEADS = 16
N_KV_HEADS = 2
GQA_GROUP = N_HEADS // N_KV_HEADS
KV_WIDTH = N_KV_HEADS * HEAD_DIM
CONV_KERNEL = 31
WINDOW = 128
ROPE_THETA = 500000.0
ROTARY_DIM = HEAD_DIM // 4
ROTARY_HALF = ROTARY_DIM // 2
RMS_EPS = 1e-6
LN_EPS = 1e-5
NEG_INF = -1e30
IN_COLS = 2 * ATTN_WIDTH + 2 * KV_WIDTH + 3 * CONV_WIDTH

COL_Q = 0
COL_KV = COL_Q + ATTN_WIDTH
COL_GATTN = COL_KV + 2 * KV_WIDTH
COL_VAL = COL_GATTN + ATTN_WIDTH
COL_GATE = COL_VAL + CONV_WIDTH
COL_GCONV = COL_GATE + CONV_WIDTH

LANES = 128
SUBLANES = 8
MXU_COLS = 256
CONV_HALO = 32
V7X_VMEM_BYTES = 64 * 1024 * 1024
VMEM_LIMIT = V7X_VMEM_BYTES // 8 * 7

F32 = jnp.float32
BF16 = jnp.bfloat16


def _prep_kernel(pos_ref, invf_ref, sel_ref, win_ref, wpw_ref, cos_ref, sin_ref, win_bf16, wpw_bf16,
                 *, chunk, rope_steps):
    i = pl.program_id(0)
    win_bf16[...] = win_ref[...].astype(BF16)

    @pl.when(i == 0)
    def _():
        wpw_bf16[...] = wpw_ref[...].astype(BF16)

    @pl.when(i < rope_steps)
    def _():
        ang = invf_ref[...] * pos_ref[...]
        cos_a, sin_a = jnp.cos(ang), jnp.sin(ang)
        sel = sel_ref[...]
        rotary = jnp.sum(sel, axis=0, keepdims=True)
        tn = (((0,), (0,)), ((), ()))
        for k in range(chunk // LANES):
            cols = slice(k * LANES, (k + 1) * LANES)
            rows = slice(k * LANES, (k + 1) * LANES)
            cos_ref[rows, :] = lax.dot_general(cos_a[:, cols], sel, tn, precision=lax.Precision.HIGHEST,
                                               preferred_element_type=F32) + (1.0 - rotary)
            sin_ref[rows, :] = lax.dot_general(sin_a[:, cols], sel, tn, precision=lax.Precision.HIGHEST,
                                               preferred_element_type=F32)


def _prep(pos, w_in, w_pw, *, chunk):
    seq = pos.shape[0]
    d, cols = w_in.shape
    rope_steps, steps = seq // chunk, cols // MXU_COLS
    assert steps >= rope_steps
    freq = jnp.arange(ROTARY_HALF, dtype=F32)
    invf = (ROPE_THETA ** (-(2 * freq) / ROTARY_DIM)).reshape(ROTARY_HALF, 1)
    lane = jnp.arange(LANES) % HEAD_DIM
    sel = ((lane[None, :] % ROTARY_HALF == jnp.arange(ROTARY_HALF)[:, None])
           & (lane[None, :] < ROTARY_DIM)).astype(F32)
    table = jax.ShapeDtypeStruct((seq, LANES), F32)

    def rope_block(i):
        return jnp.minimum(i, rope_steps - 1)

    return pl.pallas_call(
        functools.partial(_prep_kernel, chunk=chunk, rope_steps=rope_steps),
        grid=(steps,),
        in_specs=[
            pl.BlockSpec((1, chunk), lambda i: (0, rope_block(i))),
            pl.BlockSpec((ROTARY_HALF, 1), lambda i: (0, 0)),
            pl.BlockSpec((ROTARY_HALF, LANES), lambda i: (0, 0)),
            pl.BlockSpec((d, MXU_COLS), lambda i: (0, i)),
            pl.BlockSpec(w_pw.shape, lambda i: (0, 0), pipeline_mode=pl.Buffered(1)),
        ],
        out_specs=[
            pl.BlockSpec((chunk, LANES), lambda i: (rope_block(i), 0)),
            pl.BlockSpec((chunk, LANES), lambda i: (rope_block(i), 0)),
            pl.BlockSpec((d, MXU_COLS), lambda i: (0, i)),
            pl.BlockSpec(w_pw.shape, lambda i: (0, 0)),
        ],
        out_shape=(table, table, jax.ShapeDtypeStruct(w_in.shape, BF16), jax.ShapeDtypeStruct(w_pw.shape, BF16)),
        compiler_params=pltpu.CompilerParams(dimension_semantics=("arbitrary",)),
        name="prep",
    )(pos.reshape(1, seq), invf, sel, w_in, w_pw)


def _rope(xc, cos_t, sin_t):
    d = lax.broadcasted_iota(jnp.int32, xc.shape, 1) % HEAD_DIM
    lo = d < ROTARY_HALF
    partner = jnp.where(lo, pltpu.roll(xc, LANES - ROTARY_HALF, 1),
                        pltpu.roll(xc, ROTARY_HALF, 1))
    return xc * cos_t + partner * jnp.where(lo, -sin_t, sin_t)


def _head_pad_variants(a):
    lo = lax.broadcasted_iota(jnp.int32, a.shape, 1) < HEAD_DIM
    r = pltpu.roll(a, HEAD_DIM, 1)
    return (jnp.where(lo, a, 0.0), jnp.where(lo, 0.0, r),
            jnp.where(lo, r, 0.0), jnp.where(lo, 0.0, a))


def _proj_mixers_kernel(x_ref, g_ref, win_ref, bin_ref, cos_ref, sin_ref, sinks_ref, wdw_ref, bdw_ref,
                        lng_ref, lnb_ref, wpw_ref, bpw_ref, u_ref,
                        hn, qbuf, kbuf, vbuf, ga_s, gc_s, hbuf, cb, cbuf, *, tile, seg, ln_rows, norm_rows, pw_rows):
    i = pl.program_id(0)
    nblk = tile // WINDOW
    pairs = GQA_GROUP // 2
    stack = GQA_GROUP * WINDOW
    lane_tiles = CONV_WIDTH // LANES

    @pl.when(i == 0)
    def _():
        kbuf[...] = jnp.zeros(kbuf.shape, BF16)
        hbuf[...] = jnp.zeros(hbuf.shape, F32)
        lane = lax.broadcasted_iota(jnp.int32, (WINDOW + tile, LANES), 1)
        for n in range(2 * N_KV_HEADS):
            vbuf[n, :, :LANES] = jnp.zeros((WINDOW + tile, LANES), BF16)
            vbuf[n, :, LANES:] = ((lane < HEAD_DIM) == (n % 2 == 0)).astype(BF16)

    for n in range(2 * N_KV_HEADS):
        kbuf[n, :WINDOW, :] = kbuf[n, tile:, :]
        vbuf[n, :WINDOW, :LANES] = vbuf[n, tile:, :LANES]
    for c in range(lane_tiles):
        hbuf[c, :CONV_HALO, :] = hbuf[c, tile:tile + CONV_HALO, :]

    for r0 in range(0, tile, norm_rows):
        x = x_ref[r0:r0 + norm_rows, :]
        ms = jnp.mean(x * x, axis=-1, keepdims=True)
        hn[r0:r0 + norm_rows, :] = (x * lax.rsqrt(ms + RMS_EPS) * g_ref[...]).astype(BF16)

    def proj(col0, width):
        return (jnp.dot(hn[...], win_ref[:, col0:col0 + width], preferred_element_type=F32)
                + bin_ref[:, col0:col0 + width])

    for k in range(CONV_WIDTH // MXU_COLS):
        val = proj(COL_VAL + k * MXU_COLS, MXU_COLS)
        gate = proj(COL_GATE + k * MXU_COLS, MXU_COLS)
        for t in range(MXU_COLS // LANES):
            hbuf[k * (MXU_COLS // LANES) + t, CONV_HALO:CONV_HALO + tile, :] = (
                val[:, t * LANES:(t + 1) * LANES] * jax.nn.sigmoid(gate[:, t * LANES:(t + 1) * LANES]))

    cos_t, sin_t = cos_ref[...], sin_ref[...]
    scale = HEAD_DIM ** -0.5
    per_chunk = MXU_COLS // LANES

    kv = proj(COL_KV, 2 * KV_WIDTH)
    k_cur = _head_pad_variants(_rope(kv[:, :KV_WIDTH], cos_t, sin_t))
    v_cur = _head_pad_variants(kv[:, KV_WIDTH:])
    for n in range(2 * N_KV_HEADS):
        kbuf[n, WINDOW:, :] = k_cur[n].astype(BF16)
        vbuf[n, WINDOW:, :LANES] = v_cur[n].astype(BF16)

    def q_chunk(k):
        q = proj(COL_Q + k * MXU_COLS, MXU_COLS)
        for t in range(per_chunk):
            c = k * per_chunk + t
            qc = (_rope(q[:, t * LANES:(t + 1) * LANES], cos_t, sin_t) * scale).astype(BF16)
            for b in range(nblk):
                dst = (b * N_KV_HEADS * pairs + c) * WINDOW
                qbuf[dst:dst + WINDOW, :] = qc[b * WINDOW:(b + 1) * WINDOW, :]

    def ga_chunk(k):
        cols = slice(k * MXU_COLS, (k + 1) * MXU_COLS)
        ga_s[:, cols] = jax.nn.silu(proj(COL_GATTN + k * MXU_COLS, MXU_COLS))

    def gc_chunk(k):
        cols = slice(k * MXU_COLS, (k + 1) * MXU_COLS)
        gc_s[:, cols] = jax.nn.silu(proj(COL_GCONV + k * MXU_COLS, MXU_COLS))

    row = lax.broadcasted_iota(jnp.int32, (stack, WINDOW), 0) % WINDOW
    tri = lax.broadcasted_iota(jnp.int32, (stack, WINDOW), 1) <= row
    nt = (((1,), (1,)), ((), ()))
    rows_of = [slice(n * WINDOW, (n + 1) * WINDOW) for n in range(GQA_GROUP)]
    even_lanes = lax.broadcasted_iota(jnp.int32, (WINDOW, LANES), 1) < HEAD_DIM
    half = pairs * WINDOW

    def scores(b, g):
        r0 = b * WINDOW
        q0 = (b * N_KV_HEADS + g) * pairs * WINDOW
        q4 = qbuf[q0:q0 + pairs * WINDOW, :]
        s2 = jnp.concatenate(
            [lax.dot_general(q4, kbuf[2 * g + e, r0:r0 + 2 * WINDOW, :], nt, preferred_element_type=F32)
             for e in range(2)], axis=0)
        s_prev = s2[:, :WINDOW]
        if b == 0:
            s_prev = jnp.where(i > 0, s_prev, NEG_INF)
        s = jnp.where(tri, s2[:, WINDOW:], s_prev)
        heads = [g * GQA_GROUP + 2 * pr + e for e in range(2) for pr in range(pairs)]
        s_max = jnp.max(s, axis=-1, keepdims=True)
        m = [jnp.maximum(s_max[rs], sinks_ref[h]) for rs, h in zip(rows_of, heads)]
        ex = jnp.concatenate([jnp.exp(s[rs] - mh) for rs, mh in zip(rows_of, m)], axis=0)
        p2 = jnp.concatenate([jnp.where(tri, 0.0, ex), jnp.where(tri, ex, 0.0)], axis=1).astype(BF16)
        return p2, [jnp.exp(sinks_ref[h] - mh) for h, mh in zip(heads, m)]

    def outputs(b, g, p2, sink_term):
        r0 = b * WINDOW
        o = (jnp.dot(p2[:half], vbuf[2 * g, r0:r0 + 2 * WINDOW, :], preferred_element_type=F32)
             + jnp.dot(p2[half:], vbuf[2 * g + 1, r0:r0 + 2 * WINDOW, :], preferred_element_type=F32))
        for pr in range(pairs):
            c = g * pairs + pr
            rs = rows_of[pr]
            denom = o[rs, LANES:] + jnp.where(even_lanes, sink_term[pr], sink_term[pairs + pr])
            ga = ga_s[r0:r0 + WINDOW, c * LANES:(c + 1) * LANES]
            u_ref[r0:r0 + WINDOW, c * LANES:(c + 1) * LANES] = (
                o[rs, :LANES] * (1.0 / denom) * ga).astype(BF16)

    chunks_per_group = ATTN_WIDTH // MXU_COLS // N_KV_HEADS
    for k in range(chunks_per_group):
        q_chunk(k)
    for k in range(chunks_per_group):
        ga_chunk(k)
    fillers = ([functools.partial(q_chunk, k) for k in range(chunks_per_group, 2 * chunks_per_group)]
               + [functools.partial(ga_chunk, k) for k in range(chunks_per_group, 2 * chunks_per_group)]
               + [functools.partial(gc_chunk, k) for k in range(CONV_WIDTH // MXU_COLS)])
    assert nblk >= 2 * chunks_per_group
    for g in range(N_KV_HEADS):
        for b in range(nblk):
            state = scores(b, g)
            if fillers:
                fillers.pop(0)()
            outputs(b, g, *state)
    for f in fillers:
        f()

    lead = CONV_HALO - (CONV_KERNEL - 1)
    for c in range(lane_tiles):
        cs = slice(c * LANES, (c + 1) * LANES)
        w_taps = [jnp.broadcast_to(wdw_ref[j:j + 1, cs], (SUBLANES, LANES)) for j in range(CONV_KERNEL)]
        bias = jnp.broadcast_to(bdw_ref[:, cs], (SUBLANES, LANES))
        for r in range(seg):
            acc = bias
            for j in range(CONV_KERNEL):
                acc = acc + hbuf[c, pl.ds(r + lead + j, SUBLANES, stride=seg), :] * w_taps[j]
            cb[c, pl.ds(r, SUBLANES, stride=seg), :] = acc

    for r0 in range(0, tile, ln_rows):
        x = jnp.concatenate([cb[c, r0:r0 + ln_rows, :] for c in range(lane_tiles)], axis=1)
        mu = jnp.mean(x, axis=-1, keepdims=True)
        xc = x - mu
        var = jnp.mean(xc * xc, axis=-1, keepdims=True)
        y = xc * lax.rsqrt(var + LN_EPS) * lng_ref[...] + lnb_ref[...]
        cbuf[r0:r0 + ln_rows, :] = jax.nn.silu(y).astype(BF16)

    for r0 in range(0, tile, pw_rows):
        rs = slice(r0, r0 + pw_rows)
        pw = jnp.dot(cbuf[rs, :], wpw_ref[...], preferred_element_type=F32) + bpw_ref[...]
        u_ref[rs, ATTN_WIDTH:] = (pw * gc_s[rs, :]).astype(BF16)


def _proj_mixers(x2d, g, w_in, b_in, cos_t, sin_t, sinks, w_dw, b_dw, ln_g, ln_b, w_pw, b_pw, *, tile):
    s, d = x2d.shape
    seg = tile // SUBLANES + 1
    assert seg % 2 == 1 and SUBLANES * seg + CONV_KERNEL - 1 <= tile + 2 * CONV_HALO

    def resident(a):
        return pl.BlockSpec(a.shape, lambda i: (0,) * a.ndim, pipeline_mode=pl.Buffered(1))

    rows = pl.BlockSpec((tile, LANES), lambda i: (i, 0))
    return pl.pallas_call(
        functools.partial(_proj_mixers_kernel, tile=tile, seg=seg, ln_rows=64, norm_rows=64, pw_rows=256),
        grid=(s // tile,),
        in_specs=[
            pl.BlockSpec((tile, d), lambda i: (i, 0)),
            resident(g), resident(w_in), resident(b_in),
            rows, rows,
            pl.BlockSpec(memory_space=pltpu.SMEM),
            resident(w_dw), resident(b_dw), resident(ln_g), resident(ln_b), resident(w_pw), resident(b_pw),
        ],
        out_specs=pl.BlockSpec((tile, D_MODEL), lambda i: (i, 0)),
        out_shape=jax.ShapeDtypeStruct((s, D_MODEL), BF16),
        scratch_shapes=[
            pltpu.VMEM((tile, d), BF16),
            pltpu.VMEM((tile * ATTN_WIDTH // LANES, LANES), BF16),
            pltpu.VMEM((2 * N_KV_HEADS, WINDOW + tile, LANES), BF16),
            pltpu.VMEM((2 * N_KV_HEADS, WINDOW + tile, 2 * LANES), BF16),
            pltpu.VMEM((tile, ATTN_WIDTH), F32),
            pltpu.VMEM((tile, CONV_WIDTH), F32),
            pltpu.VMEM((CONV_WIDTH // LANES, tile + 2 * CONV_HALO, LANES), F32),
            pltpu.VMEM((CONV_WIDTH // LANES, SUBLANES * seg, LANES), F32),
            pltpu.VMEM((tile, CONV_WIDTH), BF16),
        ],
        compiler_params=pltpu.CompilerParams(
            dimension_semantics=("arbitrary",), vmem_limit_bytes=VMEM_LIMIT),
        name="proj_mixers",
    )(x2d, g, w_in, b_in, cos_t, sin_t, sinks, w_dw, b_dw, ln_g, ln_b, w_pw, b_pw)


def _out_proj_kernel(u_ref, x_ref, w_ref, b_ref, g_ref, o_ref, w_bf16, *, rows, cast_rows):
    @pl.when(pl.program_id(0) == 0)
    def _():
        for k0 in range(0, w_ref.shape[0], cast_rows):
            w_bf16[k0:k0 + cast_rows, :] = w_ref[k0:k0 + cast_rows, :].astype(BF16)

    for r0 in range(0, u_ref.shape[0], rows):
        rs = slice(r0, r0 + rows)
        y = jnp.dot(u_ref[rs, :], w_bf16[...], preferred_element_type=F32) + b_ref[...]
        ms = jnp.mean(y * y, axis=-1, keepdims=True)
        o_ref[rs, :] = x_ref[rs, :] + y * lax.rsqrt(ms + RMS_EPS) * g_ref[...]


def _out_proj(u, x2d, w, b, g, *, tm, rows):
    s, d = x2d.shape
    return pl.pallas_call(
        functools.partial(_out_proj_kernel, rows=rows, cast_rows=128),
        grid=(s // tm,),
        in_specs=[
            pl.BlockSpec((tm, u.shape[1]), lambda i: (i, 0)),
            pl.BlockSpec((tm, d), lambda i: (i, 0)),
            pl.BlockSpec(w.shape, lambda i: (0, 0), pipeline_mode=pl.Buffered(1)),
            pl.BlockSpec((1, d), lambda i: (0, 0)),
            pl.BlockSpec((1, d), lambda i: (0, 0)),
        ],
        out_specs=pl.BlockSpec((tm, d), lambda i: (i, 0)),
        out_shape=jax.ShapeDtypeStruct((s, d), F32),
        scratch_shapes=[pltpu.VMEM(w.shape, BF16)],
        compiler_params=pltpu.CompilerParams(
            dimension_semantics=("arbitrary",), vmem_limit_bytes=VMEM_LIMIT),
        name="out_proj",
    )(u, x2d, w, b, g)


def kernel(x, positions, pre_norm_g, w_in, b_in, sinks, w_dw, b_dw, conv_ln_g, conv_ln_b,
           w_pw, b_pw, w_out, b_out, post_norm_g):
    bsz, seq, d = x.shape
    depth = w_in.shape[0]
    assert d == D_MODEL and w_in.shape[2] == IN_COLS and seq % 1024 == 0

    outs = []
    for bi in range(bsz):
        xb = x[bi]
        for l in range(depth):
            cos_t, sin_t, w_in_b, w_pw_b = _prep(positions[bi].astype(F32), w_in[l], w_pw[l], chunk=1024)
            u = _proj_mixers(xb, pre_norm_g[l].reshape(1, d), w_in_b, b_in[l].reshape(1, IN_COLS),
                             cos_t, sin_t, sinks[l], w_dw[l], b_dw[l].reshape(1, -1),
                             conv_ln_g[l].reshape(1, -1), conv_ln_b[l].reshape(1, -1),
                             w_pw_b, b_pw[l].reshape(1, -1), tile=512)
            xb = _out_proj(u, xb, w_out[l], b_out[l].reshape(1, d),
                           post_norm_g[l].reshape(1, d), tm=512, rows=512)
        outs.append(xb)
    return outs[0].reshape(1, seq, d) if bsz == 1 else jnp.stack(outs, axis=0)
```

```python
import functools

import jax
import jax.numpy as jnp
from jax import lax
from jax.experimental import pallas as pl
from jax.experimental.pallas import tpu as pltpu

D_MODEL = 2048
CONV_WIDTH = 1024
ATTN_WIDTH = 1024
HEAD_DIM = 64
N_HEADS = 16
N_KV_HEADS = 2
GQA_GROUP = N_HEADS // N_KV_HEADS
KV_WIDTH = N_KV_HEADS * HEAD_DIM
CONV_KERNEL = 31
WINDOW = 128
ROPE_THETA = 500000.0
ROTARY_DIM = HEAD_DIM // 4
ROTARY_HALF = ROTARY_DIM // 2
RMS_EPS = 1e-6
LN_EPS = 1e-5
NEG_INF = -1e30
IN_COLS = 2 * ATTN_WIDTH + 2 * KV_WIDTH + 3 * CONV_WIDTH

COL_Q = 0
COL_KV = COL_Q + ATTN_WIDTH
COL_GATTN = COL_KV + 2 * KV_WIDTH
COL_VAL = COL_GATTN + ATTN_WIDTH
COL_GATE = COL_VAL + CONV_WIDTH
COL_GCONV = COL_GATE + CONV_WIDTH

LANES = 128
SUBLANES = 8
MXU_COLS = 256
CONV_HALO = 32
V7X_VMEM_BYTES = 64 * 1024 * 1024
VMEM_LIMIT = V7X_VMEM_BYTES // 8 * 7

F32 = jnp.float32
BF16 = jnp.bfloat16


def _prep_kernel(pos_ref, invf_ref, sel_ref, win_ref, wpw_ref, cos_ref, sin_ref, win_bf16, wpw_bf16,
                 *, chunk, rope_steps):
    i = pl.program_id(0)
    win_bf16[...] = win_ref[...].astype(BF16)

    @pl.when(i == 0)
    def _():
        wpw_bf16[...] = wpw_ref[...].astype(BF16)

    @pl.when(i < rope_steps)
    def _():
        ang = invf_ref[...] * pos_ref[...]
        cos_a, sin_a = jnp.cos(ang), jnp.sin(ang)
        sel = sel_ref[...]
        rotary = jnp.sum(sel, axis=0, keepdims=True)
        tn = (((0,), (0,)), ((), ()))
        for k in range(chunk // LANES):
            cols = slice(k * LANES, (k + 1) * LANES)
            rows = slice(k * LANES, (k + 1) * LANES)
            cos_ref[rows, :] = lax.dot_general(cos_a[:, cols], sel, tn, precision=lax.Precision.HIGHEST,
                                               preferred_element_type=F32) + (1.0 - rotary)
            sin_ref[rows, :] = lax.dot_general(sin_a[:, cols], sel, tn, precision=lax.Precision.HIGHEST,
                                               preferred_element_type=F32)


def _prep(pos, w_in, w_pw, *, chunk):
    seq = pos.shape[0]
    d, cols = w_in.shape
    rope_steps, steps = seq // chunk, cols // MXU_COLS
    assert steps >= rope_steps
    freq = jnp.arange(ROTARY_HALF, dtype=F32)
    invf = (ROPE_THETA ** (-(2 * freq) / ROTARY_DIM)).reshape(ROTARY_HALF, 1)
    lane = jnp.arange(LANES) % HEAD_DIM
    sel = ((lane[None, :] % ROTARY_HALF == jnp.arange(ROTARY_HALF)[:, None])
           & (lane[None, :] < ROTARY_DIM)).astype(F32)
    table = jax.ShapeDtypeStruct((seq, LANES), F32)

    def rope_block(i):
        return jnp.minimum(i, rope_steps - 1)

    return pl.pallas_call(
        functools.partial(_prep_kernel, chunk=chunk, rope_steps=rope_steps),
        grid=(steps,),
        in_specs=[
            pl.BlockSpec((1, chunk), lambda i: (0, rope_block(i))),
            pl.BlockSpec((ROTARY_HALF, 1), lambda i: (0, 0)),
            pl.BlockSpec((ROTARY_HALF, LANES), lambda i: (0, 0)),
            pl.BlockSpec((d, MXU_COLS), lambda i: (0, i)),
            pl.BlockSpec(w_pw.shape, lambda i: (0, 0), pipeline_mode=pl.Buffered(1)),
        ],
        out_specs=[
            pl.BlockSpec((chunk, LANES), lambda i: (rope_block(i), 0)),
            pl.BlockSpec((chunk, LANES), lambda i: (rope_block(i), 0)),
            pl.BlockSpec((d, MXU_COLS), lambda i: (0, i)),
            pl.BlockSpec(w_pw.shape, lambda i: (0, 0)),
        ],
        out_shape=(table, table, jax.ShapeDtypeStruct(w_in.shape, BF16), jax.ShapeDtypeStruct(w_pw.shape, BF16)),
        compiler_params=pltpu.CompilerParams(dimension_semantics=("arbitrary",)),
        name="prep",
    )(pos.reshape(1, seq), invf, sel, w_in, w_pw)


def _rope(xc, cos_t, sin_t):
    d = lax.broadcasted_iota(jnp.int32, xc.shape, 1) % HEAD_DIM
    lo = d < ROTARY_HALF
    partner = jnp.where(lo, pltpu.roll(xc, LANES - ROTARY_HALF, 1),
                        pltpu.roll(xc, ROTARY_HALF, 1))
    return xc * cos_t + partner * jnp.where(lo, -sin_t, sin_t)


def _head_pad_variants(a):
    lo = lax.broadcasted_iota(jnp.int32, a.shape, 1) < HEAD_DIM
    r = pltpu.roll(a, HEAD_DIM, 1)
    return (jnp.where(lo, a, 0.0), jnp.where(lo, 0.0, r),
            jnp.where(lo, r, 0.0), jnp.where(lo, 0.0, a))


def _proj_mixers_kernel(x_ref, g_ref, win_ref, bin_ref, cos_ref, sin_ref, sinks_ref, wdw_ref, bdw_ref,
                        lng_ref, lnb_ref, wpw_ref, bpw_ref, u_ref,
                        hn, qbuf, kbuf, vbuf, ga_s, gc_s, hbuf, cb, cbuf, *, tile, seg, ln_rows, norm_rows, pw_rows):
    i = pl.program_id(0)
    nblk = tile // WINDOW
    pairs = GQA_GROUP // 2
    stack = GQA_GROUP * WINDOW
    lane_tiles = CONV_WIDTH // LANES

    @pl.when(i == 0)
    def _():
        kbuf[...] = jnp.zeros(kbuf.shape, BF16)
        hbuf[...] = jnp.zeros(hbuf.shape, F32)
        lane = lax.broadcasted_iota(jnp.int32, (WINDOW + tile, LANES), 1)
        for n in range(2 * N_KV_HEADS):
            vbuf[n, :, :LANES] = jnp.zeros((WINDOW + tile, LANES), BF16)
            vbuf[n, :, LANES:] = ((lane < HEAD_DIM) == (n % 2 == 0)).astype(BF16)

    for n in range(2 * N_KV_HEADS):
        kbuf[n, :WINDOW, :] = kbuf[n, tile:, :]
        vbuf[n, :WINDOW, :LANES] = vbuf[n, tile:, :LANES]
    for c in range(lane_tiles):
        hbuf[c, :CONV_HALO, :] = hbuf[c, tile:tile + CONV_HALO, :]

    for r0 in range(0, tile, norm_rows):
        x = x_ref[r0:r0 + norm_rows, :]
        ms = jnp.mean(x * x, axis=-1, keepdims=True)
        hn[r0:r0 + norm_rows, :] = (x * lax.rsqrt(ms + RMS_EPS) * g_ref[...]).astype(BF16)

    def proj(col0, width):
        return (jnp.dot(hn[...], win_ref[:, col0:col0 + width], preferred_element_type=F32)
                + bin_ref[:, col0:col0 + width])

    for k in range(CONV_WIDTH // MXU_COLS):
        val = proj(COL_VAL + k * MXU_COLS, MXU_COLS)
        gate = proj(COL_GATE + k * MXU_COLS, MXU_COLS)
        for t in range(MXU_COLS // LANES):
            hbuf[k * (MXU_COLS // LANES) + t, CONV_HALO:CONV_HALO + tile, :] = (
                val[:, t * LANES:(t + 1) * LANES] * jax.nn.sigmoid(gate[:, t * LANES:(t + 1) * LANES]))

    cos_t, sin_t = cos_ref[...], sin_ref[...]
    scale = HEAD_DIM ** -0.5
    per_chunk = MXU_COLS // LANES

    kv = proj(COL_KV, 2 * KV_WIDTH)
    k_cur = _head_pad_variants(_rope(kv[:, :KV_WIDTH], cos_t, sin_t))
    v_cur = _head_pad_variants(kv[:, KV_WIDTH:])
    for n in range(2 * N_KV_HEADS):
        kbuf[n, WINDOW:, :] = k_cur[n].astype(BF16)
        vbuf[n, WINDOW:, :LANES] = v_cur[n].astype(BF16)

    def q_chunk(k):
        q = proj(COL_Q + k * MXU_COLS, MXU_COLS)
        for t in range(per_chunk):
            c = k * per_chunk + t
            qc = (_rope(q[:, t * LANES:(t + 1) * LANES], cos_t, sin_t) * scale).astype(BF16)
            for b in range(nblk):
                dst = (b * N_KV_HEADS * pairs + c) * WINDOW
                qbuf[dst:dst + WINDOW, :] = qc[b * WINDOW:(b + 1) * WINDOW, :]

    def ga_chunk(k):
        cols = slice(k * MXU_COLS, (k + 1) * MXU_COLS)
        ga_s[:, cols] = jax.nn.silu(proj(COL_GATTN + k * MXU_COLS, MXU_COLS))

    def gc_chunk(k):
        cols = slice(k * MXU_COLS, (k + 1) * MXU_COLS)
        gc_s[:, cols] = jax.nn.silu(proj(COL_GCONV + k * MXU_COLS, MXU_COLS))

    row = lax.broadcasted_iota(jnp.int32, (stack, WINDOW), 0) % WINDOW
    tri = lax.broadcasted_iota(jnp.int32, (stack, WINDOW), 1) <= row
    nt = (((1,), (1,)), ((), ()))
    rows_of = [slice(n * WINDOW, (n + 1) * WINDOW) for n in range(GQA_GROUP)]
    even_lanes = lax.broadcasted_iota(jnp.int32, (WINDOW, LANES), 1) < HEAD_DIM
    half = pairs * WINDOW

    def scores(b, g):
        r0 = b * WINDOW
        q0 = (b * N_KV_HEADS + g) * pairs * WINDOW
        q4 = qbuf[q0:q0 + pairs * WINDOW, :]
        s2 = jnp.concatenate(
            [lax.dot_general(q4, kbuf[2 * g + e, r0:r0 + 2 * WINDOW, :], nt, preferred_element_type=F32)
             for e in range(2)], axis=0)
        s_prev = s2[:, :WINDOW]
        if b == 0:
            s_prev = jnp.where(i > 0, s_prev, NEG_INF)
        s = jnp.where(tri, s2[:, WINDOW:], s_prev)
        heads = [g * GQA_GROUP + 2 * pr + e for e in range(2) for pr in range(pairs)]
        s_max = jnp.max(s, axis=-1, keepdims=True)
        m = [jnp.maximum(s_max[rs], sinks_ref[h]) for rs, h in zip(rows_of, heads)]
        ex = jnp.concatenate([jnp.exp(s[rs] - mh) for rs, mh in zip(rows_of, m)], axis=0)
        p2 = jnp.concatenate([jnp.where(tri, 0.0, ex), jnp.where(tri, ex, 0.0)], axis=1).astype(BF16)
        return p2, [jnp.exp(sinks_ref[h] - mh) for h, mh in zip(heads, m)]

    def outputs(b, g, p2, sink_term):
        r0 = b * WINDOW
        o = (jnp.dot(p2[:half], vbuf[2 * g, r0:r0 + 2 * WINDOW, :], preferred_element_type=F32)
             + jnp.dot(p2[half:], vbuf[2 * g + 1, r0:r0 + 2 * WINDOW, :], preferred_element_type=F32))
        for pr in range(pairs):
            c = g * pairs + pr
            rs = rows_of[pr]
            denom = o[rs, LANES:] + jnp.where(even_lanes, sink_term[pr], sink_term[pairs + pr])
            ga = ga_s[r0:r0 + WINDOW, c * LANES:(c + 1) * LANES]
            u_ref[r0:r0 + WINDOW, c * LANES:(c + 1) * LANES] = (
                o[rs, :LANES] * (1.0 / denom) * ga).astype(BF16)

    chunks_per_group = ATTN_WIDTH // MXU_COLS // N_KV_HEADS
    for k in range(chunks_per_group):
        q_chunk(k)
    for k in range(chunks_per_group):
        ga_chunk(k)
    fillers = ([functools.partial(q_chunk, k) for k in range(chunks_per_group, 2 * chunks_per_group)]
               + [functools.partial(ga_chunk, k) for k in range(chunks_per_group, 2 * chunks_per_group)]
               + [functools.partial(gc_chunk, k) for k in range(CONV_WIDTH // MXU_COLS)])
    assert nblk >= 2 * chunks_per_group
    units = [(b, g) for g in range(N_KV_HEADS) for b in range(nblk)]
    state = scores(*units[0])
    for n, unit in enumerate(units):
        ahead = scores(*units[n + 1]) if n + 1 < len(units) else None
        if fillers:
            fillers.pop(0)()
        outputs(*unit, *state)
        state = ahead
    for f in fillers:
        f()

    lead = CONV_HALO - (CONV_KERNEL - 1)
    for c in range(lane_tiles):
        cs = slice(c * LANES, (c + 1) * LANES)
        w_taps = [jnp.broadcast_to(wdw_ref[j:j + 1, cs], (SUBLANES, LANES)) for j in range(CONV_KERNEL)]
        bias = jnp.broadcast_to(bdw_ref[:, cs], (SUBLANES, LANES))
        for r in range(seg):
            acc = bias
            for j in range(CONV_KERNEL):
                acc = acc + hbuf[c, pl.ds(r + lead + j, SUBLANES, stride=seg), :] * w_taps[j]
            cb[c, pl.ds(r, SUBLANES, stride=seg), :] = acc

    for r0 in range(0, tile, ln_rows):
        x = jnp.concatenate([cb[c, r0:r0 + ln_rows, :] for c in range(lane_tiles)], axis=1)
        mu = jnp.mean(x, axis=-1, keepdims=True)
        xc = x - mu
        var = jnp.mean(xc * xc, axis=-1, keepdims=True)
        y = xc * lax.rsqrt(var + LN_EPS) * lng_ref[...] + lnb_ref[...]
        cbuf[r0:r0 + ln_rows, :] = jax.nn.silu(y).astype(BF16)

    for r0 in range(0, tile, pw_rows):
        rs = slice(r0, r0 + pw_rows)
        pw = jnp.dot(cbuf[rs, :], wpw_ref[...], preferred_element_type=F32) + bpw_ref[...]
        u_ref[rs, ATTN_WIDTH:] = (pw * gc_s[rs, :]).astype(BF16)


def _proj_mixers(x2d, g, w_in, b_in, cos_t, sin_t, sinks, w_dw, b_dw, ln_g, ln_b, w_pw, b_pw, *, tile):
    s, d = x2d.shape
    seg = tile // SUBLANES + 1
    assert seg % 2 == 1 and SUBLANES * seg + CONV_KERNEL - 1 <= tile + 2 * CONV_HALO

    def resident(a):
        return pl.BlockSpec(a.shape, lambda i: (0,) * a.ndim, pipeline_mode=pl.Buffered(1))

    rows = pl.BlockSpec((tile, LANES), lambda i: (i, 0))
    return pl.pallas_call(
        functools.partial(_proj_mixers_kernel, tile=tile, seg=seg, ln_rows=64, norm_rows=64, pw_rows=256),
        grid=(s // tile,),
        in_specs=[
            pl.BlockSpec((tile, d), lambda i: (i, 0)),
            resident(g), resident(w_in), resident(b_in),
            rows, rows,
            pl.BlockSpec(memory_space=pltpu.SMEM),
            resident(w_dw), resident(b_dw), resident(ln_g), resident(ln_b), resident(w_pw), resident(b_pw),
        ],
        out_specs=pl.BlockSpec((tile, D_MODEL), lambda i: (i, 0)),
        out_shape=jax.ShapeDtypeStruct((s, D_MODEL), BF16),
        scratch_shapes=[
            pltpu.VMEM((tile, d), BF16),
            pltpu.VMEM((tile * ATTN_WIDTH // LANES, LANES), BF16),
            pltpu.VMEM((2 * N_KV_HEADS, WINDOW + tile, LANES), BF16),
            pltpu.VMEM((2 * N_KV_HEADS, WINDOW + tile, 2 * LANES), BF16),
            pltpu.VMEM((tile, ATTN_WIDTH), F32),
            pltpu.VMEM((tile, CONV_WIDTH), F32),
            pltpu.VMEM((CONV_WIDTH // LANES, tile + 2 * CONV_HALO, LANES), F32),
            pltpu.VMEM((CONV_WIDTH // LANES, SUBLANES * seg, LANES), F32),
            pltpu.VMEM((tile, CONV_WIDTH), BF16),
        ],
        compiler_params=pltpu.CompilerParams(
            dimension_semantics=("arbitrary",), vmem_limit_bytes=VMEM_LIMIT),
        name="proj_mixers",
    )(x2d, g, w_in, b_in, cos_t, sin_t, sinks, w_dw, b_dw, ln_g, ln_b, w_pw, b_pw)


def _out_proj_kernel(u_ref, x_ref, w_ref, b_ref, g_ref, o_ref, w_bf16, *, rows, cast_rows):
    @pl.when(pl.program_id(0) == 0)
    def _():
        for k0 in range(0, w_ref.shape[0], cast_rows):
            w_bf16[k0:k0 + cast_rows, :] = w_ref[k0:k0 + cast_rows, :].astype(BF16)

    for r0 in range(0, u_ref.shape[0], rows):
        rs = slice(r0, r0 + rows)
        y = jnp.dot(u_ref[rs, :], w_bf16[...], preferred_element_type=F32) + b_ref[...]
        ms = jnp.mean(y * y, axis=-1, keepdims=True)
        o_ref[rs, :] = x_ref[rs, :] + y * lax.rsqrt(ms + RMS_EPS) * g_ref[...]


def _out_proj(u, x2d, w, b, g, *, tm, rows):
    s, d = x2d.shape
    return pl.pallas_call(
        functools.partial(_out_proj_kernel, rows=rows, cast_rows=128),
        grid=(s // tm,),
        in_specs=[
            pl.BlockSpec((tm, u.shape[1]), lambda i: (i, 0)),
            pl.BlockSpec((tm, d), lambda i: (i, 0)),
            pl.BlockSpec(w.shape, lambda i: (0, 0), pipeline_mode=pl.Buffered(1)),
            pl.BlockSpec((1, d), lambda i: (0, 0)),
            pl.BlockSpec((1, d), lambda i: (0, 0)),
        ],
        out_specs=pl.BlockSpec((tm, d), lambda i: (i, 0)),
        out_shape=jax.ShapeDtypeStruct((s, d), F32),
        scratch_shapes=[pltpu.VMEM(w.shape, BF16)],
        compiler_params=pltpu.CompilerParams(
            dimension_semantics=("arbitrary",), vmem_limit_bytes=VMEM_LIMIT),
        name="out_proj",
    )(u, x2d, w, b, g)


def kernel(x, positions, pre_norm_g, w_in, b_in, sinks, w_dw, b_dw, conv_ln_g, conv_ln_b,
           w_pw, b_pw, w_out, b_out, post_norm_g):
    bsz, seq, d = x.shape
    depth = w_in.shape[0]
    assert d == D_MODEL and w_in.shape[2] == IN_COLS and seq % 1024 == 0

    outs = []
    for bi in range(bsz):
        xb = x[bi]
        for l in range(depth):
            cos_t, sin_t, w_in_b, w_pw_b = _prep(positions[bi].astype(F32), w_in[l], w_pw[l], chunk=1024)
            u = _proj_mixers(xb, pre_norm_g[l].reshape(1, d), w_in_b, b_in[l].reshape(1, IN_COLS),
                             cos_t, sin_t, sinks[l], w_dw[l], b_dw[l].reshape(1, -1),
                             conv_ln_g[l].reshape(1, -1), conv_ln_b[l].reshape(1, -1),
                             w_pw_b, b_pw[l].reshape(1, -1), tile=512)
            xb = _out_proj(u, xb, w_out[l], b_out[l].reshape(1, d),
                           post_norm_g[l].reshape(1, d), tm=512, rows=512)
        outs.append(xb)
    return outs[0].reshape(1, seq, d) if bsz == 1 else jnp.stack(outs, axis=0)
```

```python
import functools

import jax
import jax.numpy as jnp
from jax import lax
from jax.experimental import pallas as pl
from jax.experimental.pallas import tpu as pltpu

D_MODEL = 2048
CONV_WIDTH = 1024
ATTN_WIDTH = 1024
HEAD_DIM = 64
N_HEADS = 16
N_KV_HEADS = 2
GQA_GROUP = N_HEADS // N_KV_HEADS
KV_WIDTH = N_KV_HEADS * HEAD_DIM
CONV_KERNEL = 31
WINDOW = 128
ROPE_THETA = 500000.0
ROTARY_DIM = HEAD_DIM // 4
ROTARY_HALF = ROTARY_DIM // 2
RMS_EPS = 1e-6
LN_EPS = 1e-5
NEG_INF = -1e30
IN_COLS = 2 * ATTN_WIDTH + 2 * KV_WIDTH + 3 * CONV_WIDTH

COL_Q = 0
COL_KV = COL_Q + ATTN_WIDTH
COL_GATTN = COL_KV + 2 * KV_WIDTH
COL_VAL = COL_GATTN + ATTN_WIDTH
COL_GATE = COL_VAL + CONV_WIDTH
COL_GCONV = COL_GATE + CONV_WIDTH

LANES = 128
SUBLANES = 8
MXU_COLS = 256
CONV_HALO = 32
V7X_VMEM_BYTES = 64 * 1024 * 1024
VMEM_LIMIT = V7X_VMEM_BYTES // 8 * 7

F32 = jnp.float32
BF16 = jnp.bfloat16


def _prep_kernel(pos_ref, invf_ref, sel_ref, win_ref, wpw_ref, cos_ref, sin_ref, win_bf16, wpw_bf16,
                 *, chunk, rope_steps):
    i = pl.program_id(0)
    win_bf16[...] = win_ref[...].astype(BF16)

    @pl.when(i == 0)
    def _():
        wpw_bf16[...] = wpw_ref[...].astype(BF16)

    @pl.when(i < rope_steps)
    def _():
        ang = invf_ref[...] * pos_ref[...]
        cos_a, sin_a = jnp.cos(ang), jnp.sin(ang)
        sel = sel_ref[...]
        rotary = jnp.sum(sel, axis=0, keepdims=True)
        tn = (((0,), (0,)), ((), ()))
        for k in range(chunk // LANES):
            cols = slice(k * LANES, (k + 1) * LANES)
            rows = slice(k * LANES, (k + 1) * LANES)
            cos_ref[rows, :] = lax.dot_general(cos_a[:, cols], sel, tn, precision=lax.Precision.HIGHEST,
                                               preferred_element_type=F32) + (1.0 - rotary)
            sin_ref[rows, :] = lax.dot_general(sin_a[:, cols], sel, tn, precision=lax.Precision.HIGHEST,
                                               preferred_element_type=F32)


def _prep(pos, w_in, w_pw, *, chunk):
    seq = pos.shape[0]
    d, cols = w_in.shape
    rope_steps, steps = seq // chunk, cols // MXU_COLS
    assert steps >= rope_steps
    freq = jnp.arange(ROTARY_HALF, dtype=F32)
    invf = (ROPE_THETA ** (-(2 * freq) / ROTARY_DIM)).reshape(ROTARY_HALF, 1)
    lane = jnp.arange(LANES) % HEAD_DIM
    sel = ((lane[None, :] % ROTARY_HALF == jnp.arange(ROTARY_HALF)[:, None])
           & (lane[None, :] < ROTARY_DIM)).astype(F32)
    table = jax.ShapeDtypeStruct((seq, LANES), F32)

    def rope_block(i):
        return jnp.minimum(i, rope_steps - 1)

    return pl.pallas_call(
        functools.partial(_prep_kernel, chunk=chunk, rope_steps=rope_steps),
        grid=(steps,),
        in_specs=[
            pl.BlockSpec((1, chunk), lambda i: (0, rope_block(i))),
            pl.BlockSpec((ROTARY_HALF, 1), lambda i: (0, 0)),
            pl.BlockSpec((ROTARY_HALF, LANES), lambda i: (0, 0)),
            pl.BlockSpec((d, MXU_COLS), lambda i: (0, i)),
            pl.BlockSpec(w_pw.shape, lambda i: (0, 0), pipeline_mode=pl.Buffered(1)),
        ],
        out_specs=[
            pl.BlockSpec((chunk, LANES), lambda i: (rope_block(i), 0)),
            pl.BlockSpec((chunk, LANES), lambda i: (rope_block(i), 0)),
            pl.BlockSpec((d, MXU_COLS), lambda i: (0, i)),
            pl.BlockSpec(w_pw.shape, lambda i: (0, 0)),
        ],
        out_shape=(table, table, jax.ShapeDtypeStruct(w_in.shape, BF16), jax.ShapeDtypeStruct(w_pw.shape, BF16)),
        compiler_params=pltpu.CompilerParams(dimension_semantics=("arbitrary",)),
        name="prep",
    )(pos.reshape(1, seq), invf, sel, w_in, w_pw)


def _rope(xc, cos_t, sin_t):
    d = lax.broadcasted_iota(jnp.int32, xc.shape, 1) % HEAD_DIM
    lo = d < ROTARY_HALF
    partner = jnp.where(lo, pltpu.roll(xc, LANES - ROTARY_HALF, 1),
                        pltpu.roll(xc, ROTARY_HALF, 1))
    return xc * cos_t + partner * jnp.where(lo, -sin_t, sin_t)


def _head_pad_variants(a):
    lo = lax.broadcasted_iota(jnp.int32, a.shape, 1) < HEAD_DIM
    r = pltpu.roll(a, HEAD_DIM, 1)
    return (jnp.where(lo, a, 0.0), jnp.where(lo, 0.0, r),
            jnp.where(lo, r, 0.0), jnp.where(lo, 0.0, a))


def _proj_mixers_kernel(x_ref, g_ref, win_ref, bin_ref, cos_ref, sin_ref, sinks_ref, wdw_ref, bdw_ref,
                        lng_ref, lnb_ref, wpw_ref, bpw_ref, u_ref,
                        hn, qbuf, kbuf, vbuf, ga_s, gc_s, hbuf, cb, cbuf, *, tile, seg, ln_rows, norm_rows, pw_rows):
    i = pl.program_id(0)
    nblk = tile // WINDOW
    pairs = GQA_GROUP // 2
    stack = GQA_GROUP * WINDOW
    lane_tiles = CONV_WIDTH // LANES

    @pl.when(i == 0)
    def _():
        kbuf[...] = jnp.zeros(kbuf.shape, BF16)
        hbuf[...] = jnp.zeros(hbuf.shape, F32)
        lane = lax.broadcasted_iota(jnp.int32, (WINDOW + tile, LANES), 1)
        for n in range(2 * N_KV_HEADS):
            vbuf[n, :, :LANES] = jnp.zeros((WINDOW + tile, LANES), BF16)
            vbuf[n, :, LANES:] = ((lane < HEAD_DIM) == (n % 2 == 0)).astype(BF16)

    for n in range(2 * N_KV_HEADS):
        kbuf[n, :WINDOW, :] = kbuf[n, tile:, :]
        vbuf[n, :WINDOW, :LANES] = vbuf[n, tile:, :LANES]
    for c in range(lane_tiles):
        hbuf[c, :CONV_HALO, :] = hbuf[c, tile:tile + CONV_HALO, :]

    for r0 in range(0, tile, norm_rows):
        x = x_ref[r0:r0 + norm_rows, :]
        ms = jnp.mean(x * x, axis=-1, keepdims=True)
        hn[r0:r0 + norm_rows, :] = (x * lax.rsqrt(ms + RMS_EPS) * g_ref[...]).astype(BF16)

    def proj(col0, width):
        return (jnp.dot(hn[...], win_ref[:, col0:col0 + width], preferred_element_type=F32)
                + bin_ref[:, col0:col0 + width])

    for k in range(CONV_WIDTH // MXU_COLS):
        val = proj(COL_VAL + k * MXU_COLS, MXU_COLS)
        gate = proj(COL_GATE + k * MXU_COLS, MXU_COLS)
        for t in range(MXU_COLS // LANES):
            hbuf[k * (MXU_COLS // LANES) + t, CONV_HALO:CONV_HALO + tile, :] = (
                val[:, t * LANES:(t + 1) * LANES] * jax.nn.sigmoid(gate[:, t * LANES:(t + 1) * LANES]))

    cos_t, sin_t = cos_ref[...], sin_ref[...]
    scale = HEAD_DIM ** -0.5
    per_chunk = MXU_COLS // LANES

    kv = proj(COL_KV, 2 * KV_WIDTH)
    k_cur = _head_pad_variants(_rope(kv[:, :KV_WIDTH], cos_t, sin_t))
    v_cur = _head_pad_variants(kv[:, KV_WIDTH:])
    for n in range(2 * N_KV_HEADS):
        kbuf[n, WINDOW:, :] = k_cur[n].astype(BF16)
        vbuf[n, WINDOW:, :LANES] = v_cur[n].astype(BF16)

    def q_chunk(k):
        q = proj(COL_Q + k * MXU_COLS, MXU_COLS)
        for t in range(per_chunk):
            c = k * per_chunk + t
            qc = (_rope(q[:, t * LANES:(t + 1) * LANES], cos_t, sin_t) * scale).astype(BF16)
            for b in range(nblk):
                dst = (b * N_KV_HEADS * pairs + c) * WINDOW
                qbuf[dst:dst + WINDOW, :] = qc[b * WINDOW:(b + 1) * WINDOW, :]

    def ga_chunk(k):
        cols = slice(k * MXU_COLS, (k + 1) * MXU_COLS)
        ga_s[:, cols] = jax.nn.silu(proj(COL_GATTN + k * MXU_COLS, MXU_COLS))

    def gc_chunk(k):
        cols = slice(k * MXU_COLS, (k + 1) * MXU_COLS)
        gc_s[:, cols] = jax.nn.silu(proj(COL_GCONV + k * MXU_COLS, MXU_COLS))

    row = lax.broadcasted_iota(jnp.int32, (stack, WINDOW), 0) % WINDOW
    tri = lax.broadcasted_iota(jnp.int32, (stack, WINDOW), 1) <= row
    nt = (((1,), (1,)), ((), ()))
    rows_of = [slice(n * WINDOW, (n + 1) * WINDOW) for n in range(GQA_GROUP)]
    even_lanes = lax.broadcasted_iota(jnp.int32, (WINDOW, LANES), 1) < HEAD_DIM
    half = pairs * WINDOW

    def scores(b, g):
        r0 = b * WINDOW
        q0 = (b * N_KV_HEADS + g) * pairs * WINDOW
        q4 = qbuf[q0:q0 + pairs * WINDOW, :]
        s2 = jnp.concatenate(
            [lax.dot_general(q4, kbuf[2 * g + e, r0:r0 + 2 * WINDOW, :], nt, preferred_element_type=F32)
             for e in range(2)], axis=0)
        s_prev = s2[:, :WINDOW]
        if b == 0:
            s_prev = jnp.where(i > 0, s_prev, NEG_INF)
        s = jnp.where(tri, s2[:, WINDOW:], s_prev)
        heads = [g * GQA_GROUP + 2 * pr + e for e in range(2) for pr in range(pairs)]
        s_max = jnp.max(s, axis=-1, keepdims=True)
        m = [jnp.maximum(s_max[rs], sinks_ref[h]) for rs, h in zip(rows_of, heads)]
        ex = jnp.concatenate([jnp.exp(s[rs] - mh) for rs, mh in zip(rows_of, m)], axis=0)
        p2 = jnp.concatenate([jnp.where(tri, 0.0, ex), jnp.where(tri, ex, 0.0)], axis=1).astype(BF16)
        return p2, [jnp.exp(sinks_ref[h] - mh) for h, mh in zip(heads, m)]

    def outputs(b, g, p2, sink_term):
        r0 = b * WINDOW
        o = (jnp.dot(p2[:half], vbuf[2 * g, r0:r0 + 2 * WINDOW, :], preferred_element_type=F32)
             + jnp.dot(p2[half:], vbuf[2 * g + 1, r0:r0 + 2 * WINDOW, :], preferred_element_type=F32))
        for pr in range(pairs):
            c = g * pairs + pr
            rs = rows_of[pr]
            denom = o[rs, LANES:] + jnp.where(even_lanes, sink_term[pr], sink_term[pairs + pr])
            ga = ga_s[r0:r0 + WINDOW, c * LANES:(c + 1) * LANES]
            u_ref[r0:r0 + WINDOW, c * LANES:(c + 1) * LANES] = (
                o[rs, :LANES] * (1.0 / denom) * ga).astype(BF16)

    chunks_per_group = ATTN_WIDTH // MXU_COLS // N_KV_HEADS
    for k in range(chunks_per_group):
        q_chunk(k)
    for k in range(chunks_per_group):
        ga_chunk(k)
    fillers = ([functools.partial(q_chunk, k) for k in range(chunks_per_group, 2 * chunks_per_group)]
               + [functools.partial(ga_chunk, k) for k in range(chunks_per_group, 2 * chunks_per_group)]
               + [functools.partial(gc_chunk, k) for k in range(CONV_WIDTH // MXU_COLS)])
    assert nblk >= 2 * chunks_per_group
    units = [(b, g) for g in range(N_KV_HEADS) for b in range(nblk)]
    state = scores(*units[0])
    for n, unit in enumerate(units):
        ahead = scores(*units[n + 1]) if n + 1 < len(units) else None
        if fillers:
            fillers.pop(0)()
        outputs(*unit, *state)
        state = ahead
    for f in fillers:
        f()

    lead = CONV_HALO - (CONV_KERNEL - 1)
    for c in range(lane_tiles):
        cs = slice(c * LANES, (c + 1) * LANES)
        w_taps = [jnp.broadcast_to(wdw_ref[j:j + 1, cs], (SUBLANES, LANES)) for j in range(CONV_KERNEL)]
        bias = jnp.broadcast_to(bdw_ref[:, cs], (SUBLANES, LANES))
        for r in range(seg):
            acc = bias
            for j in range(CONV_KERNEL):
                acc = acc + hbuf[c, pl.ds(r + lead + j, SUBLANES, stride=seg), :] * w_taps[j]
            cb[c, pl.ds(r, SUBLANES, stride=seg), :] = acc

    for r0 in range(0, tile, ln_rows):
        x = jnp.concatenate([cb[c, r0:r0 + ln_rows, :] for c in range(lane_tiles)], axis=1)
        mu = jnp.mean(x, axis=-1, keepdims=True)
        xc = x - mu
        var = jnp.mean(xc * xc, axis=-1, keepdims=True)
        y = xc * lax.rsqrt(var + LN_EPS) * lng_ref[...] + lnb_ref[...]
        cbuf[r0:r0 + ln_rows, :] = jax.nn.silu(y).astype(BF16)

    for r0 in range(0, tile, pw_rows):
        rs = slice(r0, r0 + pw_rows)
        pw = jnp.dot(cbuf[rs, :], wpw_ref[...], preferred_element_type=F32) + bpw_ref[...]
        u_ref[rs, ATTN_WIDTH:] = (pw * gc_s[rs, :]).astype(BF16)


def _proj_mixers(x2d, g, w_in, b_in, cos_t, sin_t, sinks, w_dw, b_dw, ln_g, ln_b, w_pw, b_pw, *, tile):
    s, d = x2d.shape
    seg = tile // SUBLANES + 1
    assert seg % 2 == 1 and SUBLANES * seg + CONV_KERNEL - 1 <= tile + 2 * CONV_HALO

    def resident(a):
        return pl.BlockSpec(a.shape, lambda i: (0,) * a.ndim, pipeline_mode=pl.Buffered(1))

    rows = pl.BlockSpec((tile, LANES), lambda i: (i, 0))
    return pl.pallas_call(
        functools.partial(_proj_mixers_kernel, tile=tile, seg=seg, ln_rows=64, norm_rows=64, pw_rows=256),
        grid=(s // tile,),
        in_specs=[
            pl.BlockSpec((tile, d), lambda i: (i, 0)),
            resident(g), resident(w_in), resident(b_in),
            rows, rows,
            pl.BlockSpec(memory_space=pltpu.SMEM),
            resident(w_dw), resident(b_dw), resident(ln_g), resident(ln_b), resident(w_pw), resident(b_pw),
        ],
        out_specs=pl.BlockSpec((tile, D_MODEL), lambda i: (i, 0)),
        out_shape=jax.ShapeDtypeStruct((s, D_MODEL), BF16),
        scratch_shapes=[
            pltpu.VMEM((tile, d), BF16),
            pltpu.VMEM((tile * ATTN_WIDTH // LANES, LANES), BF16),
            pltpu.VMEM((2 * N_KV_HEADS, WINDOW + tile, LANES), BF16),
            pltpu.VMEM((2 * N_KV_HEADS, WINDOW + tile, 2 * LANES), BF16),
            pltpu.VMEM((tile, ATTN_WIDTH), F32),
            pltpu.VMEM((tile, CONV_WIDTH), F32),
            pltpu.VMEM((CONV_WIDTH // LANES, tile + 2 * CONV_HALO, LANES), F32),
            pltpu.VMEM((CONV_WIDTH // LANES, SUBLANES * seg, LANES), F32),
            pltpu.VMEM((tile, CONV_WIDTH), BF16),
        ],
        compiler_params=pltpu.CompilerParams(
            dimension_semantics=("arbitrary",), vmem_limit_bytes=VMEM_LIMIT),
        name="proj_mixers",
    )(x2d, g, w_in, b_in, cos_t, sin_t, sinks, w_dw, b_dw, ln_g, ln_b, w_pw, b_pw)


def _out_proj_kernel(u_ref, x_ref, w_ref, b_ref, g_ref, o_ref, w_bf16, *, cast_rows):
    @pl.when(pl.program_id(0) == 0)
    def _():
        for k0 in range(0, w_ref.shape[0], cast_rows):
            w_bf16[k0:k0 + cast_rows, :] = w_ref[k0:k0 + cast_rows, :].astype(BF16)

    y = jnp.dot(u_ref[...], w_bf16[...], preferred_element_type=F32) + b_ref[...]
    ms = jnp.mean(y * y, axis=-1, keepdims=True)
    o_ref[...] = x_ref[...] + y * lax.rsqrt(ms + RMS_EPS) * g_ref[...]


def _out_proj(u, x2d, w, b, g, *, tm):
    s, d = x2d.shape
    return pl.pallas_call(
        functools.partial(_out_proj_kernel, cast_rows=128),
        grid=(s // tm,),
        in_specs=[
            pl.BlockSpec((tm, u.shape[1]), lambda i: (i, 0)),
            pl.BlockSpec((tm, d), lambda i: (i, 0)),
            pl.BlockSpec(w.shape, lambda i: (0, 0), pipeline_mode=pl.Buffered(1)),
            pl.BlockSpec((1, d), lambda i: (0, 0)),
            pl.BlockSpec((1, d), lambda i: (0, 0)),
        ],
        out_specs=pl.BlockSpec((tm, d), lambda i: (i, 0)),
        out_shape=jax.ShapeDtypeStruct((s, d), F32),
        scratch_shapes=[pltpu.VMEM(w.shape, BF16)],
        compiler_params=pltpu.CompilerParams(
            dimension_semantics=("arbitrary",), vmem_limit_bytes=VMEM_LIMIT),
        name="out_proj",
    )(u, x2d, w, b, g)


def kernel(x, positions, pre_norm_g, w_in, b_in, sinks, w_dw, b_dw, conv_ln_g, conv_ln_b,
           w_pw, b_pw, w_out, b_out, post_norm_g):
    bsz, seq, d = x.shape
    depth = w_in.shape[0]
    assert d == D_MODEL and w_in.shape[2] == IN_COLS and seq % 1024 == 0

    outs = []
    for bi in range(bsz):
        xb = x[bi]
        for l in range(depth):
            cos_t, sin_t, w_in_b, w_pw_b = _prep(positions[bi].astype(F32), w_in[l], w_pw[l], chunk=2048)
            u = _proj_mixers(xb, pre_norm_g[l].reshape(1, d), w_in_b, b_in[l].reshape(1, IN_COLS),
                             cos_t, sin_t, sinks[l], w_dw[l], b_dw[l].reshape(1, -1),
                             conv_ln_g[l].reshape(1, -1), conv_ln_b[l].reshape(1, -1),
                             w_pw_b, b_pw[l].reshape(1, -1), tile=512)
            xb = _out_proj(u, xb, w_out[l], b_out[l].reshape(1, d),
                           post_norm_g[l].reshape(1, d), tm=512)
        outs.append(xb)
    return outs[0].reshape(1, seq, d) if bsz == 1 else jnp.stack(outs, axis=0)
```
